```python
import jax
import jax.numpy as jnp
from jax import lax
import numpy as np

D_MODEL = 4096
BATCH = 4
SEQ = 2048
DEPTH = 4
DEC_BATCH = 8
DEC_SEQ = 8
PAST_LEN = 8192
PAGE_SIZE = 128

N_MIXERS = 3
N_CONF = (DEPTH + 2) // N_MIXERS
N_FOX = (DEPTH + 1) // N_MIXERS
N_GDN = DEPTH // N_MIXERS
CONF_WIDTH = 31
FOX_HEAD_DIM = 128
FOX_HEADS = D_MODEL // FOX_HEAD_DIM
FOX_WIDTH = FOX_HEADS * FOX_HEAD_DIM
Q_BLOCK = 128
FOX_GATE_BIAS_INIT = 3.0
GDN_DK = 128
GDN_DV = 128
GDN_HEADS = D_MODEL // GDN_DV
GDN_QKV = GDN_HEADS * (2 * GDN_DK + GDN_DV)
GDN_CONV = 4
GDN_CHUNK = 64
MEM_TOKENS = 256
X_HEADS = 4
X_HEAD_DIM = 128
X_WIDTH = X_HEADS * X_HEAD_DIM
FFN_DIM = ((8 * D_MODEL // 3 + 255) // 256) * 256
FFN_CONV = 3
EPS = 1e-6

kernel_name = 'hybrid_conformer_fox_gdn_decode_step'


def rmsnorm(x, g):
    xf = x.astype(jnp.float32)
    y = xf * lax.rsqrt(jnp.mean(xf * xf, axis=-1, keepdims=True) + EPS)
    return (y * g.astype(jnp.float32)).astype(x.dtype)


def layernorm(x, g, b):
    xf = x.astype(jnp.float32)
    xc = xf - jnp.mean(xf, axis=-1, keepdims=True)
    y = xc * lax.rsqrt(jnp.mean(xc * xc, axis=-1, keepdims=True) + EPS)
    return (y * g.astype(jnp.float32) + b.astype(jnp.float32)).astype(x.dtype)


def l2norm(x):
    xf = x.astype(jnp.float32)
    return xf * lax.rsqrt(jnp.sum(xf * xf, axis=-1, keepdims=True) + EPS)


def extend_with_buffer(u, buf, width):
    if buf is None:
        ext = jnp.pad(u, ((0, 0), (width - 1, 0), (0, 0)))
    else:
        ext = jnp.concatenate([buf.astype(u.dtype), u], axis=1)
    return ext, ext[:, ext.shape[1] - (width - 1):]


def causal_dwconv(ext, w):
    return lax.conv_general_dilated(ext, w[:, None, :].astype(ext.dtype), (1,), 'VALID',
                                    dimension_numbers=('NWC', 'WIO', 'NWC'),
                                    feature_group_count=ext.shape[-1])


def conformer_conv_module(h, buf, w_pw1, w_dw, ln_g, ln_b, w_pw2):
    a = h @ w_pw1
    u = a[..., :D_MODEL] * jax.nn.sigmoid(a[..., D_MODEL:])
    ext, new_buf = extend_with_buffer(u, buf, CONF_WIDTH)
    y = jax.nn.silu(layernorm(causal_dwconv(ext, w_dw), ln_g, ln_b))
    return y @ w_pw2, new_buf


def fox_project(h, w_in, b_f):
    B, T, _ = h.shape
    p = h @ w_in
    q = p[..., :FOX_WIDTH].reshape(B, T, FOX_HEADS, FOX_HEAD_DIM)
    k = p[..., FOX_WIDTH:2 * FOX_WIDTH].reshape(B, T, FOX_HEADS, FOX_HEAD_DIM)
    v = p[..., 2 * FOX_WIDTH:3 * FOX_WIDTH].reshape(B, T, FOX_HEADS, FOX_HEAD_DIM)
    logf = jax.nn.log_sigmoid((p[..., 3 * FOX_WIDTH:] + b_f).astype(jnp.float32))
    return q, k, v, logf


def fox_prompt(q, k, v, logf):
    B, T, H, Dh = q.shape
    scale = Dh ** -0.5
    c = jnp.cumsum(logf, axis=1).transpose(0, 2, 1)
    kpos = jnp.arange(T)

    def block(i):
        start = i * Q_BLOCK
        q_i = lax.dynamic_slice_in_dim(q, start, Q_BLOCK, axis=1)
        c_i = lax.dynamic_slice_in_dim(c, start, Q_BLOCK, axis=2)
        s = jnp.einsum('bqhd,bkhd->bhqk', q_i, k).astype(jnp.float32) * scale
        s = s + (c_i[..., :, None] - c[..., None, :])
        qpos = start + jnp.arange(Q_BLOCK)
        s = jnp.where(kpos[None, :] <= qpos[:, None], s, -jnp.inf)
        p = jax.nn.softmax(s, axis=-1)
        return jnp.einsum('bhqk,bkhd->bqhd', p.astype(v.dtype), v)

    o = lax.map(block, jnp.arange(T // Q_BLOCK))
    return o.transpose(1, 0, 2, 3, 4).reshape(B, T, H * Dh)


def fox_sample(q, k, v, logf, k_past, v_past, logf_past):
    B, T, H, Dh = q.shape
    P = k_past.shape[1]
    scale = Dh ** -0.5
    cum_past = jnp.cumsum(logf_past.astype(jnp.float32), axis=1)
    c_past = (cum_past - cum_past[:, -1:]).transpose(0, 2, 1)
    c_new = jnp.cumsum(logf, axis=1).transpose(0, 2, 1)
    s_past = jnp.einsum('bqhd,bkhd->bhqk', q, k_past).astype(jnp.float32) * scale
    s_past = s_past + (c_new[..., :, None] - c_past[..., None, :])
    s_new = jnp.einsum('bqhd,bkhd->bhqk', q, k).astype(jnp.float32) * scale
    s_new = s_new + (c_new[..., :, None] - c_new[..., None, :])
    s_new = jnp.where(jnp.tril(jnp.ones((T, T), bool)), s_new, -jnp.inf)
    p = jax.nn.softmax(jnp.concatenate([s_past, s_new], axis=-1), axis=-1)
    o = (jnp.einsum('bhqk,bkhd->bqhd', p[..., :P].astype(v_past.dtype), v_past)
         + jnp.einsum('bhqk,bkhd->bqhd', p[..., P:].astype(v.dtype), v))
    return o.reshape(B, T, H * Dh)


def gated_delta_chunked(q, k, v, g, beta, S0):
    B, T, H, DK = q.shape
    DV = v.shape[-1]
    C = GDN_CHUNK
    pad = (-T) % C
    Tp = T + pad
    N = Tp // C

    def blocks(a):
        a = jnp.pad(a, [(0, 0), (0, pad)] + [(0, 0)] * (a.ndim - 2))
        a = a.reshape((B, N, C) + a.shape[2:])
        return a.transpose((1, 0, 3, 2) + tuple(range(4, a.ndim)))

    qc, kc, vc = blocks(q), blocks(k), blocks(v)
    gc = jnp.cumsum(blocks(g), axis=-1)
    bc = blocks(beta)
    tril = jnp.tril(jnp.ones((C, C), bool))
    strict = jnp.tril(jnp.ones((C, C), bool), -1)
    decay = jnp.exp(jnp.where(tril, gc[..., :, None] - gc[..., None, :], -jnp.inf))
    kb = kc * bc[..., None]
    L = jnp.where(strict, jnp.einsum('nbhid,nbhjd->nbhij', kb, kc) * decay, 0.0)
    a_mat = L + jnp.eye(C, dtype=jnp.float32)
    rhs = jnp.concatenate([vc * bc[..., None], kb * jnp.exp(gc)[..., None]], axis=-1)
    sol = lax.linalg.triangular_solve(a_mat, rhs, left_side=True, lower=True, unit_diagonal=True)
    u, w = sol[..., :DV], sol[..., DV:]
    intra = jnp.einsum('nbhid,nbhjd->nbhij', qc, kc) * decay
    q_dec = qc * jnp.exp(gc)[..., None]
    k_dec = kc * jnp.exp(gc[..., -1:] - gc)[..., None]
    g_last = jnp.exp(gc[..., -1])

    def step(S, inp):
        u_i, w_i, q_i, k_i, a_i, gl = inp
        v_new = u_i - jnp.einsum('bhck,bhkv->bhcv', w_i, S)
        o = jnp.einsum('bhck,bhkv->bhcv', q_i, S) + jnp.einsum('bhij,bhjv->bhiv', a_i, v_new)
        S = S * gl[..., None, None] + jnp.einsum('bhck,bhcv->bhkv', k_i, v_new)
        return S, o

    S, o = lax.scan(step, S0.astype(jnp.float32), (u, w, q_dec, k_dec, intra, g_last))
    o = o.transpose(1, 0, 3, 2, 4).reshape(B, Tp, H, DV)[:, :T]
    return o, S


def gated_deltanet(h, buf, S0, w_in, w_conv, a_log, dt_bias, norm_g, w_o):
    B, T, _ = h.shape
    hk = GDN_HEADS * GDN_DK
    hv = GDN_HEADS * GDN_DV
    p = h @ w_in
    ext, new_buf = extend_with_buffer(p[..., :GDN_QKV], buf, GDN_CONV)
    qkv = jax.nn.silu(causal_dwconv(ext, w_conv))
    q = l2norm(qkv[..., :hk].reshape(B, T, GDN_HEADS, GDN_DK)) * (GDN_DK ** -0.5)
    k = l2norm(qkv[..., hk:2 * hk].reshape(B, T, GDN_HEADS, GDN_DK))
    v = qkv[..., 2 * hk:].reshape(B, T, GDN_HEADS, GDN_DV).astype(jnp.float32)
    z = p[..., GDN_QKV:GDN_QKV + hv].reshape(B, T, GDN_HEADS, GDN_DV)
    gates = p[..., GDN_QKV + hv:].astype(jnp.float32)
    beta = jax.nn.sigmoid(gates[..., :GDN_HEADS])
    g = -jnp.exp(a_log.astype(jnp.float32)) * jax.nn.softplus(gates[..., GDN_HEADS:] + dt_bias.astype(jnp.float32))
    if S0 is None:
        S0 = jnp.zeros((B, GDN_HEADS, GDN_DK, GDN_DV), jnp.float32)
    o, S = gated_delta_chunked(q, k, v, g, beta, S0)
    o = rmsnorm(o, norm_g) * jax.nn.silu(z.astype(jnp.float32))
    return o.reshape(B, T, hv).astype(h.dtype) @ w_o, S, new_buf


def cross_attention(h, mk, mv, w_q, w_o):
    B, T, _ = h.shape
    q = (h @ w_q).reshape(B, T, X_HEADS, X_HEAD_DIM)
    s = jnp.einsum('bqhd,bmhd->bhqm', q, mk).astype(jnp.float32) * (X_HEAD_DIM ** -0.5)
    p = jax.nn.softmax(s, axis=-1)
    o = jnp.einsum('bhqm,bmhd->bqhd', p.astype(mv.dtype), mv).reshape(B, T, X_WIDTH)
    return o @ w_o


def conv_ffn(h, buf, w_up, w_dw, w_down):
    u = h @ w_up
    ext, new_buf = extend_with_buffer(u[..., :FFN_DIM], buf, FFN_CONV)
    y = jax.nn.silu(causal_dwconv(ext, w_dw)) * u[..., FFN_DIM:]
    return y @ w_down, new_buf


def run_trunk(x, mem_k, mem_v, conf_buf, gdn_state, gdn_buf, ffn_buf, fox_past, W):
    new_conf, new_k, new_v, new_lf, new_S, new_gbuf, new_ffn = [], [], [], [], [], [], []
    for i in range(DEPTH):
        kind, slot = i % N_MIXERS, i // N_MIXERS
        h = rmsnorm(x, W['norm_mix'][i])
        if kind == 0:
            y, buf = conformer_conv_module(h, None if conf_buf is None else conf_buf[slot],
                                           W['conf_w_pw1'][slot], W['conf_w_dw'][slot], W['conf_ln_g'][slot],
                                           W['conf_ln_b'][slot], W['conf_w_pw2'][slot])
            new_conf.append(buf)
        elif kind == 1:
            q, k, v, lf = fox_project(h, W['fox_w_in'][slot], W['fox_b_f'][slot])
            if fox_past is None:
                o = fox_prompt(q, k, v, lf)
            else:
                cache_k, cache_v, cache_lf, page_table = fox_past
                nb = page_table.shape[0]
                k_past = cache_k[slot, page_table].reshape(nb, -1, FOX_HEADS, FOX_HEAD_DIM)
                v_past = cache_v[slot, page_table].reshape(nb, -1, FOX_HEADS, FOX_HEAD_DIM)
                lf_past = cache_lf[slot, page_table].reshape(nb, -1, FOX_HEADS)
                o = fox_sample(q, k, v, lf, k_past, v_past, lf_past)
            y = o @ W['fox_w_o'][slot]
            new_k.append(k)
            new_v.append(v)
            new_lf.append(lf)
        else:
            y, S, buf = gated_deltanet(h, None if gdn_buf is None else gdn_buf[slot],
                                       None if gdn_state is None else gdn_state[slot],
                                       W['gdn_w_in'][slot], W['gdn_w_conv'][slot], W['gdn_a_log'][slot],
                                       W['gdn_dt_bias'][slot], W['gdn_norm_g'][slot], W['gdn_w_o'][slot])
            new_S.append(S)
            new_gbuf.append(buf)
        x = x + y
        x = x + cross_attention(rmsnorm(x, W['norm_mem'][i]), mem_k[i], mem_v[i], W['x_w_q'][i], W['x_w_o'][i])
        y, buf = conv_ffn(rmsnorm(x, W['norm_ffn'][i]), None if ffn_buf is None else ffn_buf[i],
                          W['ffn_w_up'][i], W['ffn_w_dw'][i], W['ffn_w_down'][i])
        x = x + y
        new_ffn.append(buf)
    out = rmsnorm(x, W['norm_final'])
    return (out, jnp.stack(new_conf), jnp.stack(new_k), jnp.stack(new_v), jnp.stack(new_lf),
            jnp.stack(new_S), jnp.stack(new_gbuf), jnp.stack(new_ffn))


def setup_inputs(seed: int = 0) -> dict:
    key = jax.random.key(seed)
    keys = list(jax.random.split(key, 48))

    def nrm(shape, scale=1.0):
        return scale * jax.random.normal(keys.pop(), shape, jnp.float32)

    def gain(shape):
        return 1.0 + nrm(shape, 0.02)

    n_pages = PAST_LEN // PAGE_SIZE
    n_used = DEC_BATCH * n_pages
    n_pool = n_used + max(1, n_used // 4)
    page_table = jax.random.permutation(keys.pop(), n_pool)[:n_used].reshape(DEC_BATCH, n_pages).astype(jnp.int32)
    a_log = jnp.log(jax.random.uniform(keys.pop(), (N_GDN, GDN_HEADS), jnp.float32, 1.0, 16.0))
    dt = jnp.exp(jax.random.uniform(keys.pop(), (N_GDN, GDN_HEADS), jnp.float32,
                                    float(np.log(1e-3)), float(np.log(1e-1))))
    dt_bias = dt + jnp.log(-jnp.expm1(-dt))
    d = D_MODEL
    return {
        'x_prompt': nrm((BATCH, SEQ, d)),
        'x_sample': nrm((DEC_BATCH, DEC_SEQ, d)),
        'cache_fox_k': nrm((N_FOX, n_pool, PAGE_SIZE, FOX_HEADS, FOX_HEAD_DIM)),
        'cache_fox_v': nrm((N_FOX, n_pool, PAGE_SIZE, FOX_HEADS, FOX_HEAD_DIM)),
        'cache_fox_logf': jax.nn.log_sigmoid(FOX_GATE_BIAS_INIT + nrm((N_FOX, n_pool, PAGE_SIZE, FOX_HEADS))),
        'cache_mem_k': nrm((DEPTH, DEC_BATCH, MEM_TOKENS, X_HEADS, X_HEAD_DIM)),
        'cache_mem_v': nrm((DEPTH, DEC_BATCH, MEM_TOKENS, X_HEADS, X_HEAD_DIM)),
        'state_conf': nrm((N_CONF, DEC_BATCH, CONF_WIDTH - 1, d), 0.5),
        'state_gdn': nrm((N_GDN, DEC_BATCH, GDN_HEADS, GDN_DK, GDN_DV), GDN_DK ** -0.5),
        'state_gdn_conv': nrm((N_GDN, DEC_BATCH, GDN_CONV - 1, GDN_QKV)),
        'state_ffn_conv': nrm((DEPTH, DEC_BATCH, FFN_CONV - 1, FFN_DIM)),
        'page_table': page_table,
        'mem_prompt': nrm((BATCH, MEM_TOKENS, d)),
        'norm_mix': gain((DEPTH, d)),
        'norm_mem': gain((DEPTH, d)),
        'norm_ffn': gain((DEPTH, d)),
        'norm_final': gain((d,)),
        'conf_w_pw1': nrm((N_CONF, d, 2 * d), d ** -0.5),
        'conf_w_dw': nrm((N_CONF, CONF_WIDTH, d), CONF_WIDTH ** -0.5),
        'conf_ln_g': gain((N_CONF, d)),
        'conf_ln_b': nrm((N_CONF, d), 0.02),
        'conf_w_pw2': nrm((N_CONF, d, d), d ** -0.5),
        'fox_w_in': nrm((N_FOX, d, 3 * FOX_WIDTH + FOX_HEADS), d ** -0.5),
        'fox_b_f': FOX_GATE_BIAS_INIT + nrm((N_FOX, FOX_HEADS), 0.5),
        'fox_w_o': nrm((N_FOX, FOX_WIDTH, d), FOX_WIDTH ** -0.5),
        'gdn_w_in': nrm((N_GDN, d, GDN_QKV + GDN_HEADS * GDN_DV + 2 * GDN_HEADS), d ** -0.5),
        'gdn_w_conv': nrm((N_GDN, GDN_CONV, GDN_QKV), GDN_CONV ** -0.5),
        'gdn_a_log': a_log,
        'gdn_dt_bias': dt_bias,
        'gdn_norm_g': gain((N_GDN, GDN_DV)),
        'gdn_w_o': nrm((N_GDN, GDN_HEADS * GDN_DV, d), (GDN_HEADS * GDN_DV) ** -0.5),
        'x_w_q': nrm((DEPTH, d, X_WIDTH), d ** -0.5),
        'x_w_kv': nrm((DEPTH, d, 2 * X_WIDTH), d ** -0.5),
        'x_w_o': nrm((DEPTH, X_WIDTH, d), X_WIDTH ** -0.5),
        'ffn_w_up': nrm((DEPTH, d, 2 * FFN_DIM), d ** -0.5),
        'ffn_w_dw': nrm((DEPTH, FFN_CONV, FFN_DIM), FFN_CONV ** -0.5),
        'ffn_w_down': nrm((DEPTH, FFN_DIM, d), FFN_DIM ** -0.5),
    }


def reference(x_prompt, x_sample, cache_fox_k, cache_fox_v, cache_fox_logf, cache_mem_k, cache_mem_v,
              state_conf, state_gdn, state_gdn_conv, state_ffn_conv, page_table, mem_prompt,
              norm_mix, norm_mem, norm_ffn, norm_final,
              conf_w_pw1, conf_w_dw, conf_ln_g, conf_ln_b, conf_w_pw2,
              fox_w_in, fox_b_f, fox_w_o,
              gdn_w_in, gdn_w_conv, gdn_a_log, gdn_dt_bias, gdn_norm_g, gdn_w_o,
              x_w_q, x_w_kv, x_w_o,
              ffn_w_up, ffn_w_dw, ffn_w_down):
    W = dict(norm_mix=norm_mix, norm_mem=norm_mem, norm_ffn=norm_ffn, norm_final=norm_final,
             conf_w_pw1=conf_w_pw1, conf_w_dw=conf_w_dw, conf_ln_g=conf_ln_g, conf_ln_b=conf_ln_b,
             conf_w_pw2=conf_w_pw2, fox_w_in=fox_w_in, fox_b_f=fox_b_f, fox_w_o=fox_w_o,
             gdn_w_in=gdn_w_in, gdn_w_conv=gdn_w_conv, gdn_a_log=gdn_a_log, gdn_dt_bias=gdn_dt_bias,
             gdn_norm_g=gdn_norm_g, gdn_w_o=gdn_w_o, x_w_q=x_w_q, x_w_o=x_w_o,
             ffn_w_up=ffn_w_up, ffn_w_dw=ffn_w_dw, ffn_w_down=ffn_w_down)
    B, M, _ = mem_prompt.shape
    kv = jnp.einsum('bmd,ldn->lbmn', mem_prompt, x_w_kv)
    mem_k_p = kv[..., :X_WIDTH].reshape(DEPTH, B, M, X_HEADS, X_HEAD_DIM)
    mem_v_p = kv[..., X_WIDTH:].reshape(DEPTH, B, M, X_HEADS, X_HEAD_DIM)

    (y_prompt, conf_p, fox_k_p, fox_v_p, fox_lf_p, gdn_p, gdn_conv_p, ffn_conv_p) = run_trunk(
        x_prompt, mem_k_p, mem_v_p, None, None, None, None, None, W)
    (y_sample, conf_s, fox_k_s, fox_v_s, fox_lf_s, gdn_s, gdn_conv_s, ffn_conv_s) = run_trunk(
        x_sample, cache_mem_k, cache_mem_v, state_conf, state_gdn, state_gdn_conv, state_ffn_conv,
        (cache_fox_k, cache_fox_v, cache_fox_logf, page_table), W)

    return (y_prompt, y_sample,
            conf_p, fox_k_p, fox_v_p, fox_lf_p, gdn_p, gdn_conv_p, mem_k_p, mem_v_p, ffn_conv_p,
            conf_s, fox_k_s, fox_v_s, fox_lf_s, gdn_s, gdn_conv_s, ffn_conv_s)
```

```python
import functools

import jax
import jax.numpy as jnp
from jax import lax
from jax.experimental import pallas as pl
from jax.experimental.pallas import tpu as pltpu

EPS = 1e-6
HEAD = 128
GDN_CHUNK = 64
SUBLANES = 8
LANES = 128
VMEM_CAP = 58 * 1024 * 1024
BF16 = jnp.bfloat16
F32 = jnp.float32


def _params(semantics, vmem_bytes):
    limit = int(min(VMEM_CAP, max(vmem_bytes * 5 // 4 + (4 << 20), 16 << 20)))
    return pltpu.CompilerParams(dimension_semantics=semantics, vmem_limit_bytes=limit)


def _nbytes(shape, dtype):
    n = 1
    for s in shape:
        n *= s
    return n * jnp.dtype(dtype).itemsize


def _silu(x):
    return x * jax.nn.sigmoid(x)


def _dot(a, b):
    return jnp.dot(a, b, preferred_element_type=F32)


def _dot_nt(a, b):
    return lax.dot_general(a, b, (((1,), (1,)), ((), ())), preferred_element_type=F32)


def _dot_tn(a, b):
    return lax.dot_general(a, b, (((0,), (0,)), ((), ())), preferred_element_type=F32)


def _rmsnorm_rows(x, g):
    return x * lax.rsqrt(jnp.mean(x * x, axis=-1, keepdims=True) + EPS) * g


def _rmsnorm_kernel(x_ref, g_ref, o_ref):
    o_ref[...] = _rmsnorm_rows(x_ref[...], g_ref[...]).astype(o_ref.dtype)


def rmsnorm(x, g, out_dtype):
    m, d = x.shape
    tm = min(m, 256)
    vmem = 2 * (_nbytes((tm, d), F32) + _nbytes((tm, d), out_dtype))
    return pl.pallas_call(
        _rmsnorm_kernel,
        grid=(m // tm,),
        in_specs=[pl.BlockSpec((tm, d), lambda i: (i, 0)),
                  pl.BlockSpec((1, d), lambda i: (0, 0))],
        out_specs=pl.BlockSpec((tm, d), lambda i: (i, 0)),
        out_shape=jax.ShapeDtypeStruct((m, d), out_dtype),
        compiler_params=_params(("parallel",), vmem),
        name="rmsnorm",
    )(x, g.reshape(1, d))


def _row_tile(m, x_dtype, k):
    cap = 1024 if jnp.dtype(x_dtype).itemsize == 2 else 512
    if k > 4096:
        cap //= 4
    tm = min(m, cap)
    assert m % tm == 0
    return tm


def _col_tile(n, cap):
    tn = min(n, cap)
    assert n % tn == 0
    return tn


def _mm_kernel(*refs, has_res):
    if has_res:
        x_ref, w_ref, res_ref, o_ref, wb_ref = refs
    else:
        x_ref, w_ref, o_ref, wb_ref = refs

    @pl.when(pl.program_id(1) == 0)
    def _():
        wb_ref[...] = w_ref[...].astype(BF16)

    acc = _dot(x_ref[...].astype(BF16), wb_ref[...])
    if has_res:
        acc = res_ref[...] + acc
    o_ref[...] = acc.astype(o_ref.dtype)


def matmul(x, w, *, col_off=0, n_out=None, res=None, out_dtype=F32, tn_cap=512):
    m, k = x.shape
    n_out = w.shape[1] - col_off if n_out is None else n_out
    tn = _col_tile(n_out, tn_cap if k <= 4096 else 256)
    assert col_off % tn == 0
    off = col_off // tn
    tm = _row_tile(m, x.dtype, k)
    in_specs = [pl.BlockSpec((tm, k), lambda n, i: (i, 0)),
                pl.BlockSpec((k, tn), lambda n, i: (0, n + off))]
    args = [x, w]
    vmem = (2 * _nbytes((tm, k), x.dtype) + 2 * _nbytes((k, tn), F32) + _nbytes((k, tn), BF16)
            + 2 * _nbytes((tm, tn), out_dtype))
    if res is not None:
        in_specs.append(pl.BlockSpec((tm, tn), lambda n, i: (i, n)))
        args.append(res)
        vmem += 2 * _nbytes((tm, tn), F32)
    return pl.pallas_call(
        functools.partial(_mm_kernel, has_res=res is not None),
        grid=(n_out // tn, m // tm),
        in_specs=in_specs,
        out_specs=pl.BlockSpec((tm, tn), lambda n, i: (i, n)),
        out_shape=jax.ShapeDtypeStruct((m, n_out), out_dtype),
        scratch_shapes=[pltpu.VMEM((k, tn), BF16)],
        compiler_params=_params(("arbitrary", "arbitrary"), vmem),
        name="matmul",
    )(*args)


def _glu_kernel(x_ref, wa_ref, wb_ref, o_ref, wab_ref, wbb_ref):
    @pl.when(pl.program_id(1) == 0)
    def _():
        wab_ref[...] = wa_ref[...].astype(BF16)
        wbb_ref[...] = wb_ref[...].astype(BF16)

    x = x_ref[...].astype(BF16)
    a = _dot(x, wab_ref[...])
    b = _dot(x, wbb_ref[...])
    o_ref[...] = a * jax.nn.sigmoid(b)


def matmul_glu(x, w):
    m, k = x.shape
    n = w.shape[1] // 2
    tn = _col_tile(n, 256)
    tm = _row_tile(m, x.dtype, k)
    nb = n // tn
    vmem = (2 * _nbytes((tm, k), x.dtype) + 4 * _nbytes((k, tn), F32) + 2 * _nbytes((k, tn), BF16)
            + 2 * _nbytes((tm, tn), F32))
    return pl.pallas_call(
        _glu_kernel,
        grid=(nb, m // tm),
        in_specs=[pl.BlockSpec((tm, k), lambda j, i: (i, 0)),
                  pl.BlockSpec((k, tn), lambda j, i: (0, j)),
                  pl.BlockSpec((k, tn), lambda j, i: (0, j + nb))],
        out_specs=pl.BlockSpec((tm, tn), lambda j, i: (i, j)),
        out_shape=jax.ShapeDtypeStruct((m, n), F32),
        scratch_shapes=[pltpu.VMEM((k, tn), BF16), pltpu.VMEM((k, tn), BF16)],
        compiler_params=_params(("arbitrary", "arbitrary"), vmem),
        name="matmul_glu",
    )(x, w, w)


def _mm_conv_kernel(*refs, width, tiles_per_seq, gated):
    if gated:
        x_ref, wg_ref, wv_ref, cw_ref, y_ref, tail_ref, wgb_ref, wvb_ref, ext_ref = refs
    else:
        x_ref, wg_ref, cw_ref, y_ref, tail_ref, wgb_ref, ext_ref = refs
    i = pl.program_id(1)

    @pl.when(i == 0)
    def _():
        wgb_ref[...] = wg_ref[...].astype(BF16)
        if gated:
            wvb_ref[...] = wv_ref[...].astype(BF16)

    @pl.when(i % tiles_per_seq == 0)
    def _():
        ext_ref[0:SUBLANES, :] = jnp.zeros((SUBLANES, ext_ref.shape[1]), F32)

    x = x_ref[...].astype(BF16)
    g = _dot(x, wgb_ref[...])
    tm = g.shape[0]
    ext_ref[SUBLANES:SUBLANES + tm, :] = g
    cw = cw_ref[...]
    y = g * cw[width - 1:width, :]
    for j in range(width - 1):
        y = y + ext_ref[pl.ds(SUBLANES - (width - 1) + j, tm), :] * cw[j:j + 1, :]
    y = _silu(y)
    if gated:
        y = y * _dot(x, wvb_ref[...])
    y_ref[...] = y.astype(y_ref.dtype)
    tail = g[tm - SUBLANES:tm, :]
    tail_ref[0] = tail
    ext_ref[0:SUBLANES, :] = tail


def matmul_conv(x, w, cw, *, n_seq, n_out, gated, out_dtype):
    m, k = x.shape
    width = cw.shape[0]
    tn = _col_tile(n_out, 256 if gated else 512)
    t = m // n_seq
    tm = _row_tile(t, x.dtype, k)
    tps = t // tm
    nb = n_out // tn
    in_specs = [pl.BlockSpec((tm, k), lambda j, i: (i, 0)),
                pl.BlockSpec((k, tn), lambda j, i: (0, j))]
    args = [x, w]
    scratch = [pltpu.VMEM((k, tn), BF16)]
    if gated:
        in_specs.append(pl.BlockSpec((k, tn), lambda j, i: (0, j + nb)))
        args.append(w)
        scratch.append(pltpu.VMEM((k, tn), BF16))
    in_specs.append(pl.BlockSpec((width, tn), lambda j, i: (0, j)))
    args.append(cw)
    scratch.append(pltpu.VMEM((tm + SUBLANES, tn), F32))
    nw = 2 if gated else 1
    vmem = (2 * _nbytes((tm, k), x.dtype) + nw * (2 * _nbytes((k, tn), F32) + _nbytes((k, tn), BF16))
            + 2 * _nbytes((tm, tn), out_dtype) + 6 * _nbytes((tm, tn), F32))
    return pl.pallas_call(
        functools.partial(_mm_conv_kernel, width=width, tiles_per_seq=tps, gated=gated),
        grid=(nb, m // tm),
        in_specs=in_specs,
        out_specs=[pl.BlockSpec((tm, tn), lambda j, i: (i, j)),
                   pl.BlockSpec((1, SUBLANES, tn), lambda j, i: (i // tps, 0, j))],
        out_shape=[jax.ShapeDtypeStruct((m, n_out), out_dtype),
                   jax.ShapeDtypeStruct((n_seq, SUBLANES, n_out), F32)],
        scratch_shapes=scratch,
        compiler_params=_params(("arbitrary", "arbitrary"), vmem),
        name="matmul_conv",
    )(*args)


def _state_conv_kernel(*refs, width, gated):
    if gated:
        u_ref, v_ref, st_ref, cw_ref, y_ref, ext_ref = refs
    else:
        u_ref, st_ref, cw_ref, y_ref, ext_ref = refs
    t = u_ref.shape[0]
    ext_ref[SUBLANES - (width - 1):SUBLANES, :] = st_ref[0]
    ext_ref[SUBLANES:SUBLANES + t, :] = u_ref[...]
    cw = cw_ref[...]
    y = jnp.zeros(u_ref.shape, F32)
    for j in range(width):
        y = y + ext_ref[pl.ds(SUBLANES - (width - 1) + j, t), :] * cw[j:j + 1, :]
    y = _silu(y)
    if gated:
        y = y * v_ref[...]
    y_ref[...] = y


def state_conv(u, state, cw, *, n_out, gated):
    n_seq, wm1, _ = state.shape
    t = u.shape[0] // n_seq
    width = wm1 + 1
    tc = _col_tile(n_out, 256)
    nb = n_out // tc
    in_specs = [pl.BlockSpec((t, tc), lambda s, j: (s, j))]
    args = [u]
    if gated:
        in_specs.append(pl.BlockSpec((t, tc), lambda s, j: (s, j + nb)))
        args.append(u)
    in_specs += [pl.BlockSpec((1, wm1, tc), lambda s, j: (s, 0, j)),
                 pl.BlockSpec((width, tc), lambda s, j: (0, j))]
    args += [state, cw]
    return pl.pallas_call(
        functools.partial(_state_conv_kernel, width=width, gated=gated),
        grid=(n_seq, nb),
        in_specs=in_specs,
        out_specs=pl.BlockSpec((t, tc), lambda s, j: (s, j)),
        out_shape=jax.ShapeDtypeStruct((n_seq * t, n_out), F32),
        scratch_shapes=[pltpu.VMEM((SUBLANES + t, tc), F32)],
        compiler_params=_params(("parallel", "parallel"), 1 << 20),
        name="state_conv",
    )(*args)


def _next_state(state, u, n_seq, n_out):
    wm1 = state.shape[1]
    ext = jnp.concatenate([state, u[:, :n_out].reshape(n_seq, -1, n_out)], axis=1)
    return ext[:, ext.shape[1] - wm1:]


CONF_HALO = 32
CONF_ROWS = 32
CONF_COLS = 256


def _conf_kernel(u_ref, halo_ref, w_ref, lg_ref, lb_ref, o_ref, ext_ref, y_ref, *,
                 taps, tiles_per_seq, halo_is_prev_rows):
    tr, d = u_ref.shape
    if halo_is_prev_rows:
        first = pl.program_id(0) % tiles_per_seq == 0

        @pl.when(first)
        def _():
            ext_ref[0:CONF_HALO, :] = jnp.zeros((CONF_HALO, d), F32)

        @pl.when(jnp.logical_not(first))
        def _():
            ext_ref[0:CONF_HALO, :] = halo_ref[...]
    else:
        ext_ref[0:CONF_HALO, :] = halo_ref[...]
    ext_ref[CONF_HALO:CONF_HALO + tr, :] = u_ref[...]

    rows = min(tr, CONF_ROWS)
    lead = CONF_HALO - (taps - 1)

    def col_block(c, carry):
        c0 = pl.multiple_of(c * CONF_COLS, CONF_COLS)
        w = w_ref[:, pl.ds(c0, CONF_COLS)]
        for r0 in range(0, tr, rows):
            acc = jnp.zeros((rows, CONF_COLS), F32)
            for r in range(min(SUBLANES, taps)):
                span = rows + (len(range(r, taps, SUBLANES)) - 1) * SUBLANES
                s = ext_ref[pl.ds(r0 + lead + r, span), pl.ds(c0, CONF_COLS)]
                for j in range(r, taps, SUBLANES):
                    acc = acc + s[j - r:j - r + rows, :] * w[j:j + 1, :]
            y_ref[r0:r0 + rows, pl.ds(c0, CONF_COLS)] = acc
        return carry

    lax.fori_loop(0, d // CONF_COLS, col_block, 0)

    def row_block(i, carry):
        r0 = pl.multiple_of(i * rows, rows)
        y = y_ref[pl.ds(r0, rows), :]
        yc = y - jnp.mean(y, axis=-1, keepdims=True)
        yn = yc * lax.rsqrt(jnp.mean(yc * yc, axis=-1, keepdims=True) + EPS)
        yn = yn * lg_ref[...] + lb_ref[...]
        o_ref[pl.ds(r0, rows), :] = _silu(yn).astype(o_ref.dtype)
        return carry

    lax.fori_loop(0, tr // rows, row_block, 0)


def conformer_conv(u, halo, w_dw, ln_g, ln_b, *, n_seq, out_dtype):
    m, d = u.shape
    taps = w_dw.shape[0]
    assert taps - 1 <= CONF_HALO and d % CONF_COLS == 0
    t = m // n_seq
    tr = min(t, 256)
    assert t % tr == 0 and tr % SUBLANES == 0
    tps = t // tr
    prev = halo is None
    if prev:
        assert tr % CONF_HALO == 0
        per = tr // CONF_HALO
        halo_arr = u
        halo_spec = pl.BlockSpec((CONF_HALO, d), lambda i: (jnp.maximum(i * per - 1, 0), 0))
    else:
        assert tps == 1
        halo_arr = halo
        halo_spec = pl.BlockSpec((CONF_HALO, d), lambda i: (i, 0))
    vmem = (2 * _nbytes((tr, d), F32) + 2 * _nbytes((CONF_HALO, d), F32) + 2 * _nbytes((tr, d), out_dtype)
            + _nbytes((2 * tr + CONF_HALO, d), F32) + 4 * _nbytes((32, d), F32))
    return pl.pallas_call(
        functools.partial(_conf_kernel, taps=taps, tiles_per_seq=tps, halo_is_prev_rows=prev),
        grid=(m // tr,),
        in_specs=[pl.BlockSpec((tr, d), lambda i: (i, 0)),
                  halo_spec,
                  pl.BlockSpec((taps, d), lambda i: (0, 0)),
                  pl.BlockSpec((1, d), lambda i: (0, 0)),
                  pl.BlockSpec((1, d), lambda i: (0, 0))],
        out_specs=pl.BlockSpec((tr, d), lambda i: (i, 0)),
        out_shape=jax.ShapeDtypeStruct((m, d), out_dtype),
        scratch_shapes=[pltpu.VMEM((CONF_HALO + tr, d), F32), pltpu.VMEM((tr, d), F32)],
        compiler_params=_params(("arbitrary",), vmem),
        name="conformer_conv",
    )(u, halo_arr, w_dw, ln_g.reshape(1, d), ln_b.reshape(1, d))


def _lane_cumsum(x, seg=None):
    n = x.shape[-1]
    lane = lax.broadcasted_iota(jnp.int32, x.shape, x.ndim - 1)
    pos = lane if seg is None else lane % seg
    d = 1
    while d < (n if seg is None else seg):
        x = x + jnp.where(pos >= d, pltpu.roll(x, d, x.ndim - 1), 0.0)
        d *= 2
    return x


def _logf_kernel(gt_ref, bf_ref, lf_ref, c_ref):
    lf = jax.nn.log_sigmoid(gt_ref[0] + bf_ref[...])
    lf_ref[0] = lf
    c_ref[0] = _lane_cumsum(lf)


def fox_logf(gates_t, b_f):
    n_seq, h, t = gates_t.shape
    spec = pl.BlockSpec((1, h, t), lambda s: (s, 0, 0))
    shape = jax.ShapeDtypeStruct((n_seq, h, t), F32)
    return pl.pallas_call(
        _logf_kernel,
        grid=(n_seq,),
        in_specs=[spec, pl.BlockSpec((h, 1), lambda s: (0, 0))],
        out_specs=[spec, spec],
        out_shape=[shape, shape],
        compiler_params=_params(("parallel",), 1 << 20),
        name="fox_logf",
    )(gates_t, b_f.reshape(h, 1))


def _flash_step(q, k_ref, v_ref, ck_ref, cq, carry, j, *, tk, scale, q0, masked):
    m, l, acc = carry
    ks = pl.multiple_of(j * tk, tk)
    k = k_ref[pl.ds(ks, tk), :].astype(BF16)
    v = v_ref[pl.ds(ks, tk), :].astype(BF16)
    s = _dot_nt(q, k) * scale
    s = s + (cq - ck_ref[0, 0, :, pl.ds(ks, tk)])
    if masked:
        qpos = q0 + lax.broadcasted_iota(jnp.int32, s.shape, 0)
        kpos = ks + lax.broadcasted_iota(jnp.int32, s.shape, 1)
        s = jnp.where(kpos <= qpos, s, -jnp.inf)
    m_new = jnp.maximum(m, jnp.max(s, axis=-1, keepdims=True))
    alpha = jnp.exp(m - m_new)
    p = jnp.exp(s - m_new)
    l = alpha * l + jnp.sum(p, axis=-1, keepdims=True)
    acc = alpha * acc + _dot(p.astype(BF16), v)
    return m_new, l, acc


def _flash_kernel(q_ref, k_ref, v_ref, cq_ref, ck_ref, o_ref, *, scale):
    tq = q_ref.shape[0]
    qi = pl.program_id(2)
    q = q_ref[...].astype(BF16)
    cq = cq_ref[0, 0]
    step = functools.partial(_flash_step, q, k_ref, v_ref, ck_ref, cq, tk=tq, scale=scale, q0=qi * tq)
    init = (jnp.full((tq, 1), -jnp.inf, F32), jnp.zeros((tq, 1), F32), jnp.zeros((tq, HEAD), F32))
    carry = lax.fori_loop(0, qi, lambda j, c: step(c, j, masked=False), init)
    m, l, acc = step(carry, qi, masked=True)
    o_ref[...] = (acc / l).astype(o_ref.dtype)


def fox_prompt_attention(q, k, v, c, *, n_seq):
    m, width = q.shape
    h = width // HEAD
    t = m // n_seq
    tq = min(t, 512)
    nq = t // tq
    cq = c.reshape(n_seq, h, t, 1)
    ck = c.reshape(n_seq, h, 1, t)
    vmem = (2 * _nbytes((tq, HEAD), F32) * 2 + 4 * _nbytes((t, HEAD), F32) + 2 * _nbytes((tq, LANES), F32)
            + 8 * _nbytes((tq, tq), F32))
    return pl.pallas_call(
        functools.partial(_flash_kernel, scale=HEAD ** -0.5),
        grid=(n_seq, h, nq),
        in_specs=[pl.BlockSpec((tq, HEAD), lambda b, hh, i: (b * nq + i, hh)),
                  pl.BlockSpec((t, HEAD), lambda b, hh, i: (b, hh)),
                  pl.BlockSpec((t, HEAD), lambda b, hh, i: (b, hh)),
                  pl.BlockSpec((1, 1, tq, 1), lambda b, hh, i: (b, hh, i, 0)),
                  pl.BlockSpec((1, 1, 1, t), lambda b, hh, i: (b, hh, 0, 0))],
        out_specs=pl.BlockSpec((tq, HEAD), lambda b, hh, i: (b * nq + i, hh)),
        out_shape=jax.ShapeDtypeStruct((m, width), BF16),
        compiler_params=_params(("parallel", "parallel", "arbitrary"), vmem),
        name="fox_prompt_attention",
    )(q, k, v, cq, ck)


def _split3(x):
    hi = x.astype(BF16)
    r = x - hi.astype(F32)
    mid = r.astype(BF16)
    lo = (r - mid.astype(F32)).astype(BF16)
    return hi, mid, lo


def _cpast_kernel(pt_ref, lf_ref, c_ref, carry_ref):
    del pt_ref
    p = pl.program_id(1)

    @pl.when(p == 0)
    def _():
        carry_ref[...] = jnp.zeros(carry_ref.shape, F32)

    x = lf_ref[0]
    n = x.shape[0]
    row = lax.broadcasted_iota(jnp.int32, (n, n), 0)
    col = lax.broadcasted_iota(jnp.int32, (n, n), 1)
    upper = jnp.where(col > row, 1.0, 0.0).astype(BF16)
    hi, mid, lo = _split3(x)
    later = (_dot(upper, hi) + _dot(upper, mid)) + _dot(upper, lo)
    c_ref[0, 0] = -(later + carry_ref[...])
    carry_ref[...] = carry_ref[...] + jnp.sum(x, axis=0, keepdims=True)


def fox_past_decay(cache_logf, page_table):
    _, page, h = cache_logf.shape
    n_seq, n_pages = page_table.shape
    grid_spec = pltpu.PrefetchScalarGridSpec(
        num_scalar_prefetch=1,
        grid=(n_seq, n_pages),
        in_specs=[pl.BlockSpec((1, page, h), lambda b, p, pt: (pt[b, n_pages - 1 - p], 0, 0))],
        out_specs=pl.BlockSpec((1, 1, page, h), lambda b, p, pt: (b, n_pages - 1 - p, 0, 0)),
        scratch_shapes=[pltpu.VMEM((1, h), F32)],
    )
    return pl.pallas_call(
        _cpast_kernel,
        grid_spec=grid_spec,
        out_shape=jax.ShapeDtypeStruct((n_seq, n_pages, page, h), F32),
        compiler_params=_params(("arbitrary", "arbitrary"), 1 << 20),
        name="fox_past_decay",
    )(page_table, cache_logf)


def _decode_attend(q_ref, k, v, bias_fn, m_ref, l_ref, acc_ref, *, heads, scale, causal):
    for hh in range(heads):
        cols = slice(hh * HEAD, (hh + 1) * HEAD)
        q = q_ref[:, cols].astype(BF16)
        s = _dot_nt(q, k[:, cols]) * scale + bias_fn(hh)
        if causal:
            qpos = lax.broadcasted_iota(jnp.int32, s.shape, 0)
            kpos = lax.broadcasted_iota(jnp.int32, s.shape, 1)
            s = jnp.where(kpos <= qpos, s, -jnp.inf)
        m = m_ref[hh]
        m_new = jnp.maximum(m, jnp.max(s, axis=-1, keepdims=True))
        alpha = jnp.exp(m - m_new)
        p = jnp.exp(s - m_new)
        l_ref[hh] = alpha * l_ref[hh] + jnp.sum(p, axis=-1, keepdims=True)
        acc_ref[:, cols] = alpha * acc_ref[:, cols] + _dot(p.astype(BF16), v[:, cols])
        m_ref[hh] = m_new


def _decode_kernel(pt_ref, q_ref, kp_ref, vp_ref, cp_ref, kn_ref, vn_ref, cn_col_ref, cn_row_ref, o_ref,
                   m_ref, l_ref, acc_ref, *, heads, scale):
    del pt_ref
    p = pl.program_id(1)

    @pl.when(p == 0)
    def _():
        m_ref[...] = jnp.full(m_ref.shape, -jnp.inf, F32)
        l_ref[...] = jnp.zeros(l_ref.shape, F32)
        acc_ref[...] = jnp.zeros(acc_ref.shape, F32)

    attend = functools.partial(_decode_attend, q_ref, m_ref=m_ref, l_ref=l_ref, acc_ref=acc_ref,
                               heads=heads, scale=scale)
    cn_col = cn_col_ref[0]

    def past_bias(hh):
        return cn_col[:, hh:hh + 1] - cp_ref[0, 0, hh:hh + 1, :]

    attend(kp_ref[0].astype(BF16), vp_ref[0].astype(BF16), past_bias, causal=False)

    @pl.when(p == pl.num_programs(1) - 1)
    def _():
        def new_bias(hh):
            return cn_col[:, hh:hh + 1] - cn_row_ref[0, hh:hh + 1, :]

        attend(kn_ref[0].astype(BF16), vn_ref[0].astype(BF16), new_bias, causal=True)
        for hh in range(heads):
            cols = slice(hh * HEAD, (hh + 1) * HEAD)
            o_ref[:, cols] = acc_ref[:, cols] / l_ref[hh]


def fox_sample_attention(q, k_new, v_new, c_new, cache_k, cache_v, c_past_t, page_table):
    n_seq, n_pages = page_table.shape
    m, width = q.shape
    h = width // HEAD
    t = m // n_seq
    page = cache_k.shape[1]
    pad = lambda a: jnp.pad(a.reshape(n_seq, t, width), ((0, 0), (0, page - t), (0, 0)))
    cn_col = jnp.transpose(c_new[:, :, :t], (0, 2, 1))
    cn_row = jnp.pad(c_new[:, :, :t], ((0, 0), (0, 0), (0, page - t)))
    grid_spec = pltpu.PrefetchScalarGridSpec(
        num_scalar_prefetch=1,
        grid=(n_seq, n_pages),
        in_specs=[pl.BlockSpec((t, width), lambda b, p, pt: (b, 0)),
                  pl.BlockSpec((1, page, width), lambda b, p, pt: (pt[b, p], 0, 0)),
                  pl.BlockSpec((1, page, width), lambda b, p, pt: (pt[b, p], 0, 0)),
                  pl.BlockSpec((1, 1, h, page), lambda b, p, pt: (b, p, 0, 0)),
                  pl.BlockSpec((1, page, width), lambda b, p, pt: (b, 0, 0)),
                  pl.BlockSpec((1, page, width), lambda b, p, pt: (b, 0, 0)),
                  pl.BlockSpec((1, t, h), lambda b, p, pt: (b, 0, 0)),
                  pl.BlockSpec((1, h, page), lambda b, p, pt: (b, 0, 0))],
        out_specs=pl.BlockSpec((t, width), lambda b, p, pt: (b, 0)),
        scratch_shapes=[pltpu.VMEM((h, t, 1), F32), pltpu.VMEM((h, t, 1), F32), pltpu.VMEM((t, width), F32)],
    )
    vmem = 8 * _nbytes((page, width), F32) + 6 * _nbytes((page, width), BF16)
    return pl.pallas_call(
        functools.partial(_decode_kernel, heads=h, scale=HEAD ** -0.5),
        grid_spec=grid_spec,
        out_shape=jax.ShapeDtypeStruct((m, width), F32),
        compiler_params=_params(("arbitrary", "arbitrary"), vmem),
        name="fox_sample_attention",
    )(page_table, q, cache_k, cache_v, c_past_t, pad(k_new), pad(v_new), cn_col, cn_row)


def _gdn_gate_kernel(gt_ref, alog_ref, dtb_ref, beta_ref, gc_ref, *, heads, t_valid):
    g_all = gt_ref[0]
    lane = lax.broadcasted_iota(jnp.int32, (heads, g_all.shape[1]), 1)
    valid = lane < t_valid
    beta = jnp.where(valid, jax.nn.sigmoid(g_all[:heads]), 0.0)
    g = -jnp.exp(alog_ref[...]) * jax.nn.softplus(g_all[heads:] + dtb_ref[...])
    g = jnp.where(valid, g, 0.0)
    beta_ref[0] = beta
    gc_ref[0] = _lane_cumsum(g, GDN_CHUNK)


def gdn_gates(gates_t, a_log, dt_bias, *, t_valid):
    n_seq, h2, t = gates_t.shape
    h = h2 // 2
    out_spec = pl.BlockSpec((1, h, t), lambda s: (s, 0, 0))
    shape = jax.ShapeDtypeStruct((n_seq, h, t), F32)
    return pl.pallas_call(
        functools.partial(_gdn_gate_kernel, heads=h, t_valid=t_valid),
        grid=(n_seq,),
        in_specs=[pl.BlockSpec((1, h2, t), lambda s: (s, 0, 0)),
                  pl.BlockSpec((h, 1), lambda s: (0, 0)),
                  pl.BlockSpec((h, 1), lambda s: (0, 0))],
        out_specs=[out_spec, out_spec],
        out_shape=[shape, shape],
        compiler_params=_params(("parallel",), 1 << 20),
        name="gdn_gates",
    )(gates_t, a_log.reshape(h, 1), dt_bias.reshape(h, 1))


def _row_to_col(row, eye):
    return jnp.sum(jnp.where(eye, row, 0.0), axis=1, keepdims=True)


def _gdn_kernel(q_ref, k_ref, v_ref, z_ref, beta_ref, gc_ref, s0_ref, ng_ref, o_ref, s_ref, *, hb, n_chunks):
    c_len = GDN_CHUNK
    row = lax.broadcasted_iota(jnp.int32, (c_len, c_len), 0)
    col = lax.broadcasted_iota(jnp.int32, (c_len, c_len), 1)
    eye = row == col
    lower = col <= row
    strict = col < row
    ng = ng_ref[...]

    def chunk(c, states):
        r0 = pl.multiple_of(c * c_len, c_len)
        new_states = []
        for hh in range(hb):
            cols = slice(hh * HEAD, (hh + 1) * HEAD)
            s_prev = states[hh]
            q = q_ref[pl.ds(r0, c_len), cols]
            k = k_ref[pl.ds(r0, c_len), cols]
            v = v_ref[pl.ds(r0, c_len), cols]
            qn = q * lax.rsqrt(jnp.sum(q * q, axis=-1, keepdims=True) + EPS) * (HEAD ** -0.5)
            kn = k * lax.rsqrt(jnp.sum(k * k, axis=-1, keepdims=True) + EPS)
            beta_row = beta_ref[0, hh, pl.ds(c, 1), :]
            gc_row = gc_ref[0, hh, pl.ds(c, 1), :]
            beta_col = _row_to_col(beta_row, eye)
            gc_col = _row_to_col(gc_row, eye)
            gc_last = gc_row[:, c_len - 1:c_len]
            decay = jnp.exp(jnp.where(lower, gc_col - gc_row, -jnp.inf))
            kn16 = kn.astype(BF16)
            lmat = jnp.where(strict, _dot_nt(kn16, kn16) * decay * beta_col, 0.0)
            inv = jnp.where(eye, 1.0, 0.0) - lmat
            power = lmat
            span = 2
            while span < c_len:
                p16 = power.astype(BF16)
                power = _dot(p16, p16)
                inv = inv + _dot(inv.astype(BF16), power.astype(BF16))
                span *= 2
            e_col = jnp.exp(gc_col)
            rhs = jnp.concatenate([v * beta_col, kn * (beta_col * e_col)], axis=1)
            sol = _dot(inv.astype(BF16), rhs.astype(BF16))
            u, w = sol[:, :HEAD], sol[:, HEAD:]
            intra = _dot_nt(qn.astype(BF16), kn16) * decay
            q_dec = qn * e_col
            k_dec = kn * jnp.exp(gc_last - gc_col)
            s16 = s_prev.astype(BF16)
            v_new = u - _dot(w.astype(BF16), s16)
            v_new16 = v_new.astype(BF16)
            o = _dot(q_dec.astype(BF16), s16) + _dot(intra.astype(BF16), v_new16)
            new_states.append(s_prev * jnp.exp(gc_last) + _dot_tn(k_dec.astype(BF16), v_new16))
            on = o * lax.rsqrt(jnp.mean(o * o, axis=-1, keepdims=True) + EPS) * ng
            z = z_ref[pl.ds(r0, c_len), cols]
            o_ref[pl.ds(r0, c_len), cols] = (on * _silu(z)).astype(o_ref.dtype)
        return tuple(new_states)

    states = lax.fori_loop(0, n_chunks, chunk, tuple(s0_ref[0, hh] for hh in range(hb)))
    for hh in range(hb):
        s_ref[0, hh] = states[hh]


def gdn_mix(qkv, z, z_col_off, beta, gc, s0, norm_g, *, n_seq, out_dtype):
    m = qkv.shape[0]
    h = qkv.shape[1] // (3 * HEAD)
    t = m // n_seq
    n_chunks = t // GDN_CHUNK
    hb = 2
    nhb = h // hb
    wblk = hb * HEAD
    zoff = z_col_off // wblk
    beta = beta.reshape(n_seq, h, n_chunks, GDN_CHUNK)
    gc = gc.reshape(n_seq, h, n_chunks, GDN_CHUNK)
    seq_spec = lambda off: pl.BlockSpec((t, wblk), lambda b, j: (b, j + off))
    gate_spec = pl.BlockSpec((1, hb, n_chunks, GDN_CHUNK), lambda b, j: (b, j, 0, 0))
    state_spec = pl.BlockSpec((1, hb, HEAD, HEAD), lambda b, j: (b, j, 0, 0))
    vmem = 2 * 5 * _nbytes((t, wblk), F32) + (8 << 20)
    return pl.pallas_call(
        functools.partial(_gdn_kernel, hb=hb, n_chunks=n_chunks),
        grid=(n_seq, nhb),
        in_specs=[seq_spec(0), seq_spec(nhb), seq_spec(2 * nhb), seq_spec(zoff),
                  gate_spec, gate_spec, state_spec,
                  pl.BlockSpec((1, HEAD), lambda b, j: (0, 0))],
        out_specs=[pl.BlockSpec((t, wblk), lambda b, j: (b, j)), state_spec],
        out_shape=[jax.ShapeDtypeStruct((m, h * HEAD), out_dtype),
                   jax.ShapeDtypeStruct((n_seq, h, HEAD, HEAD), F32)],
        compiler_params=_params(("parallel", "parallel"), vmem),
        name="gdn_mix",
    )(qkv, qkv, qkv, z, beta, gc, s0, norm_g.reshape(1, HEAD))


def _xattn_q_kernel(x_ref, w_ref, mk_ref, mv_ref, o_ref, wb_ref, *, heads, scale):
    @pl.when(pl.program_id(0) == 0)
    def _():
        wb_ref[...] = w_ref[...].astype(BF16)

    q = _dot(x_ref[...].astype(BF16), wb_ref[...])
    for hh in range(heads):
        cols = slice(hh * HEAD, (hh + 1) * HEAD)
        s = _dot_nt(q[:, cols].astype(BF16), mk_ref[0, :, cols].astype(BF16)) * scale
        p = jnp.exp(s - jnp.max(s, axis=-1, keepdims=True))
        o = _dot(p.astype(BF16), mv_ref[0, :, cols].astype(BF16)) / jnp.sum(p, axis=-1, keepdims=True)
        o_ref[:, cols] = o.astype(o_ref.dtype)


def xattn_heads(x, w_q, mem_k, mem_v, *, n_seq):
    m, d = x.shape
    width = w_q.shape[1]
    t = m // n_seq
    tm = min(t, _row_tile(m, x.dtype, d))
    tps = t // tm
    mem = mem_k.shape[1]
    vmem = (2 * _nbytes((tm, d), x.dtype) + 3 * _nbytes((d, width), F32) + 4 * _nbytes((mem, width), F32)
            + 2 * _nbytes((tm, width), F32) + 8 * _nbytes((tm, mem), F32))
    return pl.pallas_call(
        functools.partial(_xattn_q_kernel, heads=width // HEAD, scale=HEAD ** -0.5),
        grid=(m // tm,),
        in_specs=[pl.BlockSpec((tm, d), lambda i: (i, 0)),
                  pl.BlockSpec((d, width), lambda i: (0, 0)),
                  pl.BlockSpec((1, mem, width), lambda i: (i // tps, 0, 0)),
                  pl.BlockSpec((1, mem, width), lambda i: (i // tps, 0, 0))],
        out_specs=pl.BlockSpec((tm, width), lambda i: (i, 0)),
        out_shape=jax.ShapeDtypeStruct((m, width), F32),
        scratch_shapes=[pltpu.VMEM((d, width), BF16)],
        compiler_params=_params(("arbitrary",), vmem),
        name="xattn_heads",
    )(x, w_q, mem_k, mem_v)


def _xattn_o_kernel(a_ref, w_ref, res_ref, g_ref, x_ref, h_ref, wb_ref):
    @pl.when(pl.program_id(0) == 0)
    def _():
        wb_ref[...] = w_ref[...].astype(BF16)

    x = res_ref[...] + _dot(a_ref[...].astype(BF16), wb_ref[...])
    x_ref[...] = x
    h_ref[...] = _rmsnorm_rows(x, g_ref[...]).astype(h_ref.dtype)


def xattn_out(a, w_o, res, g, h_dtype):
    m, width = a.shape
    d = w_o.shape[1]
    tm = min(m, 256)
    vmem = (2 * _nbytes((tm, width), F32) + 3 * _nbytes((width, d), F32) + 4 * _nbytes((tm, d), F32)
            + 2 * _nbytes((tm, d), h_dtype) + 2 * _nbytes((tm, d), F32))
    return pl.pallas_call(
        _xattn_o_kernel,
        grid=(m // tm,),
        in_specs=[pl.BlockSpec((tm, width), lambda i: (i, 0)),
                  pl.BlockSpec((width, d), lambda i: (0, 0)),
                  pl.BlockSpec((tm, d), lambda i: (i, 0)),
                  pl.BlockSpec((1, d), lambda i: (0, 0))],
        out_specs=[pl.BlockSpec((tm, d), lambda i: (i, 0)), pl.BlockSpec((tm, d), lambda i: (i, 0))],
        out_shape=[jax.ShapeDtypeStruct((m, d), F32), jax.ShapeDtypeStruct((m, d), h_dtype)],
        scratch_shapes=[pltpu.VMEM((width, d), BF16)],
        compiler_params=_params(("arbitrary",), vmem),
        name="xattn_out",
    )(a, w_o, res, g.reshape(1, d))


def _pad_lanes(a, mult):
    pad = (-a.shape[-1]) % mult
    return a if pad == 0 else jnp.pad(a, [(0, 0)] * (a.ndim - 1) + [(0, pad)])


def _trunk(x, n_seq, mem_k, mem_v, states, W):
    m, d = x.shape
    t = m // n_seq
    fresh = states is None
    act = BF16 if fresh else F32
    depth = W['norm_mix'].shape[0]
    ffn_dim = W['ffn_w_dw'].shape[2]
    heads = d // HEAD
    new_conf, new_k, new_v, new_lf, new_s, new_gbuf, new_ffn = [], [], [], [], [], [], []
    for i in range(depth):
        kind, slot = i % 3, i // 3
        h = rmsnorm(x, W['norm_mix'][i], act)
        if kind == 0:
            u = matmul_glu(h, W['conf_w_pw1'][slot])
            taps = W['conf_w_dw'].shape[1]
            if fresh:
                halo = None
                new_conf.append(u.reshape(n_seq, t, d)[:, t - (taps - 1):])
            else:
                buf = states['conf'][slot]
                halo = jnp.pad(buf, ((0, 0), (CONF_HALO - (taps - 1), 0), (0, 0))).reshape(n_seq * CONF_HALO, d)
                new_conf.append(_next_state(buf, u, n_seq, d))
            y = conformer_conv(u, halo, W['conf_w_dw'][slot], W['conf_ln_g'][slot], W['conf_ln_b'][slot],
                               n_seq=n_seq, out_dtype=act)
            x = matmul(y, W['conf_w_pw2'][slot], res=x)
        elif kind == 1:
            w_in = W['fox_w_in'][slot]
            fw = heads * HEAD
            q = matmul(h, w_in, col_off=0, n_out=fw)
            k = matmul(h, w_in, col_off=fw, n_out=fw)
            v = matmul(h, w_in, col_off=2 * fw, n_out=fw)
            gate = matmul(h, w_in[:, 3 * fw:])
            gates_t = _pad_lanes(jnp.transpose(gate.reshape(n_seq, t, heads), (0, 2, 1)), LANES)
            lf_t, c_t = fox_logf(gates_t, W['fox_b_f'][slot])
            if fresh:
                o = fox_prompt_attention(q, k, v, c_t, n_seq=n_seq)
            else:
                cache_k, cache_v, cache_lf, page_table = states['fox']
                pool, page = cache_k.shape[1], cache_k.shape[2]
                c_past = fox_past_decay(cache_lf[slot], page_table)
                c_past_t = jnp.transpose(c_past, (0, 1, 3, 2))
                o = fox_sample_attention(q, k, v, c_t, cache_k[slot].reshape(pool, page, fw),
                                         cache_v[slot].reshape(pool, page, fw), c_past_t, page_table)
            x = matmul(o, W['fox_w_o'][slot], res=x)
            new_k.append(k.reshape(n_seq, t, heads, HEAD))
            new_v.append(v.reshape(n_seq, t, heads, HEAD))
            new_lf.append(jnp.transpose(lf_t[:, :, :t], (0, 2, 1)))
        else:
            w_in = W['gdn_w_in'][slot]
            cw = W['gdn_w_conv'][slot]
            n_qkv = cw.shape[1]
            hv = heads * HEAD
            if fresh:
                qkv, tails = matmul_conv(h, w_in, cw, n_seq=n_seq, n_out=n_qkv, gated=False, out_dtype=F32)
                new_gbuf.append(tails[:, SUBLANES - (cw.shape[0] - 1):])
                zsrc, zoff = matmul(h, w_in, col_off=n_qkv, n_out=hv), 0
                s0 = jnp.zeros((n_seq, heads, HEAD, HEAD), F32)
                t_pad = t
            else:
                pre = matmul(h, w_in, col_off=0, n_out=n_qkv + hv)
                buf = states['gdn_conv'][slot]
                qkv = state_conv(pre, buf, cw, n_out=n_qkv, gated=False)
                new_gbuf.append(_next_state(buf, pre, n_seq, n_qkv))
                t_pad = -(-t // GDN_CHUNK) * GDN_CHUNK
                rows = lambda a: jnp.pad(a.reshape(n_seq, t, -1), ((0, 0), (0, t_pad - t), (0, 0))).reshape(
                    n_seq * t_pad, -1)
                qkv = rows(qkv)
                zsrc, zoff = rows(pre[:, n_qkv:]), 0
                s0 = states['gdn'][slot]
            gate = matmul(h, w_in[:, n_qkv + hv:])
            gates_t = jnp.transpose(gate.reshape(n_seq, t, 2 * heads), (0, 2, 1))
            gates_t = jnp.pad(gates_t, ((0, 0), (0, 0), (0, max(t_pad, LANES) - t)))
            beta, gc = gdn_gates(gates_t, W['gdn_a_log'][slot], W['gdn_dt_bias'][slot], t_valid=t)
            o, s_new = gdn_mix(qkv, zsrc, zoff, beta[:, :, :t_pad], gc[:, :, :t_pad], s0, W['gdn_norm_g'][slot],
                               n_seq=n_seq, out_dtype=act)
            if t_pad != t:
                o = o.reshape(n_seq, t_pad, hv)[:, :t].reshape(m, hv)
            x = matmul(o, W['gdn_w_o'][slot], res=x)
            new_s.append(s_new)
        h = rmsnorm(x, W['norm_mem'][i], act)
        a = xattn_heads(h, W['x_w_q'][i], mem_k[i], mem_v[i], n_seq=n_seq)
        x, h = xattn_out(a, W['x_w_o'][i], x, W['norm_ffn'][i], act)
        w_up = W['ffn_w_up'][i]
        cw = W['ffn_w_dw'][i]
        if fresh:
            y, tails = matmul_conv(h, w_up, cw, n_seq=n_seq, n_out=ffn_dim, gated=True, out_dtype=act)
            new_ffn.append(tails[:, SUBLANES - (cw.shape[0] - 1):])
        else:
            pre = matmul(h, w_up)
            buf = states['ffn_conv'][i]
            y = state_conv(pre, buf, cw, n_out=ffn_dim, gated=True)
            new_ffn.append(_next_state(buf, pre, n_seq, ffn_dim))
        x = matmul(y, W['ffn_w_down'][i], res=x)
    out = rmsnorm(x, W['norm_final'], F32)
    return (out.reshape(n_seq, t, d), jnp.stack(new_conf), jnp.stack(new_k), jnp.stack(new_v), jnp.stack(new_lf),
            jnp.stack(new_s), jnp.stack(new_gbuf), jnp.stack(new_ffn))


def kernel(x_prompt, x_sample, cache_fox_k, cache_fox_v, cache_fox_logf, cache_mem_k, cache_mem_v, state_conf,
           state_gdn, state_gdn_conv, state_ffn_conv, page_table, mem_prompt, norm_mix, norm_mem, norm_ffn,
           norm_final, conf_w_pw1, conf_w_dw, conf_ln_g, conf_ln_b, conf_w_pw2, fox_w_in, fox_b_f, fox_w_o,
           gdn_w_in, gdn_w_conv, gdn_a_log, gdn_dt_bias, gdn_norm_g, gdn_w_o, x_w_q, x_w_kv, x_w_o, ffn_w_up,
           ffn_w_dw, ffn_w_down):
    W = dict(norm_mix=norm_mix, norm_mem=norm_mem, norm_ffn=norm_ffn, norm_final=norm_final,
             conf_w_pw1=conf_w_pw1, conf_w_dw=conf_w_dw, conf_ln_g=conf_ln_g, conf_ln_b=conf_ln_b,
             conf_w_pw2=conf_w_pw2, fox_w_in=fox_w_in, fox_b_f=fox_b_f, fox_w_o=fox_w_o,
             gdn_w_in=gdn_w_in, gdn_w_conv=gdn_w_conv, gdn_a_log=gdn_a_log, gdn_dt_bias=gdn_dt_bias,
             gdn_norm_g=gdn_norm_g, gdn_w_o=gdn_w_o, x_w_q=x_w_q, x_w_o=x_w_o,
             ffn_w_up=ffn_w_up, ffn_w_dw=ffn_w_dw, ffn_w_down=ffn_w_down)
    b, t, d = x_prompt.shape
    bs, ts, _ = x_sample.shape
    depth = x_w_kv.shape[0]
    mem = mem_prompt.shape[1]
    xw = x_w_kv.shape[2] // 2
    xh = xw // HEAD

    mem_rows = mem_prompt.reshape(b * mem, d)
    kv = jnp.stack([matmul(mem_rows, x_w_kv[i]) for i in range(depth)])
    mem_k_p = kv[..., :xw].reshape(depth, b, mem, xw)
    mem_v_p = kv[..., xw:].reshape(depth, b, mem, xw)

    (y_p, conf_p, fox_k_p, fox_v_p, fox_lf_p, gdn_p, gdn_conv_p, ffn_conv_p) = _trunk(
        x_prompt.reshape(b * t, d), b, mem_k_p, mem_v_p, None, W)

    states = dict(conf=state_conf, gdn=state_gdn, gdn_conv=state_gdn_conv, ffn_conv=state_ffn_conv,
                  fox=(cache_fox_k, cache_fox_v, cache_fox_logf, page_table))
    (y_s, conf_s, fox_k_s, fox_v_s, fox_lf_s, gdn_s, gdn_conv_s, ffn_conv_s) = _trunk(
        x_sample.reshape(bs * ts, d), bs, cache_mem_k.reshape(depth, bs, mem, xw),
        cache_mem_v.reshape(depth, bs, mem, xw), states, W)

    return (y_p, y_s,
            conf_p, fox_k_p, fox_v_p, fox_lf_p, gdn_p, gdn_conv_p,
            mem_k_p.reshape(depth, b, mem, xh, HEAD), mem_v_p.reshape(depth, b, mem, xh, HEAD), ffn_conv_p,
            conf_s, fox_k_s, fox_v_s, fox_lf_s, gdn_s, gdn_conv_s, ffn_conv_s)
```

```python
import functools

import jax
import jax.numpy as jnp
from jax import lax
from jax.experimental import pallas as pl
from jax.experimental.pallas import tpu as pltpu

EPS = 1e-6
HEAD = 128
GDN_CHUNK = 64
SUBLANES = 8
LANES = 128
VMEM_CAP = 58 * 1024 * 1024
BF16 = jnp.bfloat16
F32 = jnp.float32


def _params(semantics, vmem_bytes):
    limit = int(min(VMEM_CAP, max(vmem_bytes * 5 // 4 + (4 << 20), 16 << 20)))
    return pltpu.CompilerParams(dimension_semantics=semantics, vmem_limit_bytes=limit)


def _nbytes(shape, dtype):
    n = 1
    for s in shape:
        n *= s
    return n * jnp.dtype(dtype).itemsize


def _silu(x):
    return x * jax.nn.sigmoid(x)


def _dot(a, b):
    return jnp.dot(a, b, preferred_element_type=F32)


def _dot_nt(a, b):
    return lax.dot_general(a, b, (((1,), (1,)), ((), ())), preferred_element_type=F32)


def _dot_tn(a, b):
    return lax.dot_general(a, b, (((0,), (0,)), ((), ())), preferred_element_type=F32)


def _rmsnorm_rows(x, g):
    return x * lax.rsqrt(jnp.mean(x * x, axis=-1, keepdims=True) + EPS) * g


def _rmsnorm_kernel(x_ref, g_ref, o_ref):
    o_ref[...] = _rmsnorm_rows(x_ref[...], g_ref[...]).astype(o_ref.dtype)


def rmsnorm(x, g, out_dtype):
    m, d = x.shape
    tm = min(m, 256)
    vmem = 2 * (_nbytes((tm, d), F32) + _nbytes((tm, d), out_dtype))
    return pl.pallas_call(
        _rmsnorm_kernel,
        grid=(m // tm,),
        in_specs=[pl.BlockSpec((tm, d), lambda i: (i, 0)),
                  pl.BlockSpec((1, d), lambda i: (0, 0))],
        out_specs=pl.BlockSpec((tm, d), lambda i: (i, 0)),
        out_shape=jax.ShapeDtypeStruct((m, d), out_dtype),
        compiler_params=_params(("parallel",), vmem),
        name="rmsnorm",
    )(x, g.reshape(1, d))


def _row_tile(m, x_dtype, k):
    cap = 1024 if jnp.dtype(x_dtype).itemsize == 2 else 512
    if k > 4096:
        cap //= 4
    tm = min(m, cap)
    assert m % tm == 0
    return tm


def _col_tile(n, cap):
    tn = min(n, cap)
    assert n % tn == 0
    return tn


def _mm_kernel(*refs, has_res):
    if has_res:
        x_ref, w_ref, res_ref, o_ref, wb_ref = refs
    else:
        x_ref, w_ref, o_ref, wb_ref = refs

    @pl.when(pl.program_id(1) == 0)
    def _():
        wb_ref[...] = w_ref[...].astype(BF16)

    acc = _dot(x_ref[...].astype(BF16), wb_ref[...])
    if has_res:
        acc = res_ref[...] + acc
    o_ref[...] = acc.astype(o_ref.dtype)


def _stacked(w, layer):
    return (w[None], 0) if w.ndim == 2 else (w, layer)


def matmul(x, w, layer=0, *, col_off=0, n_out=None, res=None, out_dtype=F32):
    w, layer = _stacked(w, layer)
    m, k = x.shape
    n_out = w.shape[2] - col_off if n_out is None else n_out
    deep = k > 4096
    tn = _col_tile(n_out, 512)
    assert col_off % tn == 0
    off = col_off // tn
    tm = _row_tile(m, x.dtype, k)
    w_bufs = 1 if deep else 2
    w_mode = dict(pipeline_mode=pl.Buffered(1)) if deep else {}
    in_specs = [pl.BlockSpec((tm, k), lambda n, i: (i, 0)),
                pl.BlockSpec((None, k, tn), lambda n, i: (layer, 0, n + off), **w_mode)]
    args = [x, w]
    vmem = (2 * _nbytes((tm, k), x.dtype) + w_bufs * _nbytes((k, tn), F32) + _nbytes((k, tn), BF16)
            + 2 * _nbytes((tm, tn), out_dtype))
    if res is not None:
        in_specs.append(pl.BlockSpec((tm, tn), lambda n, i: (i, n)))
        args.append(res)
        vmem += 2 * _nbytes((tm, tn), F32)
    return pl.pallas_call(
        functools.partial(_mm_kernel, has_res=res is not None),
        grid=(n_out // tn, m // tm),
        in_specs=in_specs,
        out_specs=pl.BlockSpec((tm, tn), lambda n, i: (i, n)),
        out_shape=jax.ShapeDtypeStruct((m, n_out), out_dtype),
        scratch_shapes=[pltpu.VMEM((k, tn), BF16)],
        compiler_params=_params(("arbitrary", "arbitrary"), vmem),
        name="matmul",
    )(*args)


def _glu_kernel(x_ref, wa_ref, wb_ref, o_ref, wab_ref, wbb_ref):
    @pl.when(pl.program_id(1) == 0)
    def _():
        wab_ref[...] = wa_ref[...].astype(BF16)
        wbb_ref[...] = wb_ref[...].astype(BF16)

    x = x_ref[...].astype(BF16)
    a = _dot(x, wab_ref[...])
    b = _dot(x, wbb_ref[...])
    o_ref[...] = a * jax.nn.sigmoid(b)


def matmul_glu(x, w, layer):
    m, k = x.shape
    n = w.shape[2] // 2
    tn = _col_tile(n, 256)
    tm = _row_tile(m, x.dtype, k)
    nb = n // tn
    vmem = (2 * _nbytes((tm, k), x.dtype) + 4 * _nbytes((k, tn), F32) + 2 * _nbytes((k, tn), BF16)
            + 2 * _nbytes((tm, tn), F32))
    return pl.pallas_call(
        _glu_kernel,
        grid=(nb, m // tm),
        in_specs=[pl.BlockSpec((tm, k), lambda j, i: (i, 0)),
                  pl.BlockSpec((None, k, tn), lambda j, i: (layer, 0, j)),
                  pl.BlockSpec((None, k, tn), lambda j, i: (layer, 0, j + nb))],
        out_specs=pl.BlockSpec((tm, tn), lambda j, i: (i, j)),
        out_shape=jax.ShapeDtypeStruct((m, n), F32),
        scratch_shapes=[pltpu.VMEM((k, tn), BF16), pltpu.VMEM((k, tn), BF16)],
        compiler_params=_params(("arbitrary", "arbitrary"), vmem),
        name="matmul_glu",
    )(x, w, w)


def _mm_conv_kernel(*refs, width, tiles_per_seq, gated):
    if gated:
        x_ref, wg_ref, wv_ref, cw_ref, y_ref, tail_ref, wgb_ref, wvb_ref, ext_ref = refs
    else:
        x_ref, wg_ref, cw_ref, y_ref, tail_ref, wgb_ref, ext_ref = refs
    i = pl.program_id(1)

    @pl.when(i == 0)
    def _():
        wgb_ref[...] = wg_ref[...].astype(BF16)
        if gated:
            wvb_ref[...] = wv_ref[...].astype(BF16)

    @pl.when(i % tiles_per_seq == 0)
    def _():
        ext_ref[0:SUBLANES, :] = jnp.zeros((SUBLANES, ext_ref.shape[1]), F32)

    x = x_ref[...].astype(BF16)
    g = _dot(x, wgb_ref[...])
    tm = g.shape[0]
    ext_ref[SUBLANES:SUBLANES + tm, :] = g
    cw = cw_ref[...]
    y = g * cw[width - 1:width, :]
    for j in range(width - 1):
        y = y + ext_ref[pl.ds(SUBLANES - (width - 1) + j, tm), :] * cw[j:j + 1, :]
    y = _silu(y)
    if gated:
        y = y * _dot(x, wvb_ref[...])
    y_ref[...] = y.astype(y_ref.dtype)
    tail = g[tm - SUBLANES:tm, :]
    tail_ref[0] = tail
    ext_ref[0:SUBLANES, :] = tail


def matmul_conv(x, w, cw, layer, *, n_seq, n_out, gated, out_dtype):
    m, k = x.shape
    width = cw.shape[1]
    tn = _col_tile(n_out, 256 if gated else 512)
    t = m // n_seq
    tm = _row_tile(t, x.dtype, k)
    tps = t // tm
    nb = n_out // tn
    in_specs = [pl.BlockSpec((tm, k), lambda j, i: (i, 0)),
                pl.BlockSpec((None, k, tn), lambda j, i: (layer, 0, j))]
    args = [x, w]
    scratch = [pltpu.VMEM((k, tn), BF16)]
    if gated:
        in_specs.append(pl.BlockSpec((None, k, tn), lambda j, i: (layer, 0, j + nb)))
        args.append(w)
        scratch.append(pltpu.VMEM((k, tn), BF16))
    in_specs.append(pl.BlockSpec((None, width, tn), lambda j, i: (layer, 0, j)))
    args.append(cw)
    scratch.append(pltpu.VMEM((tm + SUBLANES, tn), F32))
    nw = 2 if gated else 1
    vmem = (2 * _nbytes((tm, k), x.dtype) + nw * (2 * _nbytes((k, tn), F32) + _nbytes((k, tn), BF16))
            + 2 * _nbytes((tm, tn), out_dtype) + 6 * _nbytes((tm, tn), F32))
    return pl.pallas_call(
        functools.partial(_mm_conv_kernel, width=width, tiles_per_seq=tps, gated=gated),
        grid=(nb, m // tm),
        in_specs=in_specs,
        out_specs=[pl.BlockSpec((tm, tn), lambda j, i: (i, j)),
                   pl.BlockSpec((1, SUBLANES, tn), lambda j, i: (i // tps, 0, j))],
        out_shape=[jax.ShapeDtypeStruct((m, n_out), out_dtype),
                   jax.ShapeDtypeStruct((n_seq, SUBLANES, n_out), F32)],
        scratch_shapes=scratch,
        compiler_params=_params(("arbitrary", "arbitrary"), vmem),
        name="matmul_conv",
    )(*args)


def _state_conv_kernel(*refs, width, gated):
    if gated:
        u_ref, v_ref, st_ref, cw_ref, y_ref, ext_ref = refs
    else:
        u_ref, st_ref, cw_ref, y_ref, ext_ref = refs
    t = u_ref.shape[0]
    ext_ref[SUBLANES - (width - 1):SUBLANES, :] = st_ref[0]
    ext_ref[SUBLANES:SUBLANES + t, :] = u_ref[...]
    cw = cw_ref[...]
    y = jnp.zeros(u_ref.shape, F32)
    for j in range(width):
        y = y + ext_ref[pl.ds(SUBLANES - (width - 1) + j, t), :] * cw[j:j + 1, :]
    y = _silu(y)
    if gated:
        y = y * v_ref[...]
    y_ref[...] = y


def state_conv(u, state, cw, *, n_out, gated):
    n_seq, wm1, _ = state.shape
    t = u.shape[0] // n_seq
    width = wm1 + 1
    tc = _col_tile(n_out, 256)
    nb = n_out // tc
    in_specs = [pl.BlockSpec((t, tc), lambda s, j: (s, j))]
    args = [u]
    if gated:
        in_specs.append(pl.BlockSpec((t, tc), lambda s, j: (s, j + nb)))
        args.append(u)
    in_specs += [pl.BlockSpec((1, wm1, tc), lambda s, j: (s, 0, j)),
                 pl.BlockSpec((width, tc), lambda s, j: (0, j))]
    args += [state, cw]
    return pl.pallas_call(
        functools.partial(_state_conv_kernel, width=width, gated=gated),
        grid=(n_seq, nb),
        in_specs=in_specs,
        out_specs=pl.BlockSpec((t, tc), lambda s, j: (s, j)),
        out_shape=jax.ShapeDtypeStruct((n_seq * t, n_out), F32),
        scratch_shapes=[pltpu.VMEM((SUBLANES + t, tc), F32)],
        compiler_params=_params(("parallel", "parallel"), 1 << 20),
        name="state_conv",
    )(*args)


def _next_state(state, u, n_seq, n_out):
    wm1 = state.shape[1]
    ext = jnp.concatenate([state, u[:, :n_out].reshape(n_seq, -1, n_out)], axis=1)
    return ext[:, ext.shape[1] - wm1:]


CONF_HALO = 32
CONF_ROWS = 64
CONF_COLS = 256


def _conf_kernel(u_ref, halo_ref, w_ref, lg_ref, lb_ref, o_ref, ext_ref, y_ref, sh_ref, *,
                 taps, tiles_per_seq, halo_is_prev_rows):
    tr, d = u_ref.shape
    if halo_is_prev_rows:
        first = pl.program_id(0) % tiles_per_seq == 0

        @pl.when(first)
        def _():
            ext_ref[0:CONF_HALO, :] = jnp.zeros((CONF_HALO, d), F32)

        @pl.when(jnp.logical_not(first))
        def _():
            ext_ref[0:CONF_HALO, :] = halo_ref[...]
    else:
        ext_ref[0:CONF_HALO, :] = halo_ref[...]
    ext_ref[CONF_HALO:CONF_HALO + tr, :] = u_ref[...]

    rows = min(tr, CONF_ROWS)
    lead = CONF_HALO - (taps - 1)

    def col_block(c, carry):
        c0 = pl.multiple_of(c * CONF_COLS, CONF_COLS)
        w = w_ref[:, pl.ds(c0, CONF_COLS)]
        for r in range(min(SUBLANES, taps)):
            span = tr + (len(range(r, taps, SUBLANES)) - 1) * SUBLANES
            sh_ref[r, 0:span, :] = ext_ref[pl.ds(lead + r, span), pl.ds(c0, CONF_COLS)]
        for r0 in range(0, tr, rows):
            acc = jnp.zeros((rows, CONF_COLS), F32)
            for j in range(taps):
                r = j % SUBLANES
                acc = acc + sh_ref[r, r0 + j - r:r0 + j - r + rows, :] * w[j:j + 1, :]
            y_ref[r0:r0 + rows, pl.ds(c0, CONF_COLS)] = acc
        return carry

    lax.fori_loop(0, d // CONF_COLS, col_block, 0)

    def row_block(i, carry):
        r0 = pl.multiple_of(i * rows, rows)
        y = y_ref[pl.ds(r0, rows), :]
        yc = y - jnp.mean(y, axis=-1, keepdims=True)
        yn = yc * lax.rsqrt(jnp.mean(yc * yc, axis=-1, keepdims=True) + EPS)
        yn = yn * lg_ref[...] + lb_ref[...]
        o_ref[pl.ds(r0, rows), :] = _silu(yn).astype(o_ref.dtype)
        return carry

    lax.fori_loop(0, tr // rows, row_block, 0)


def conformer_conv(u, halo, w_dw, ln_g, ln_b, *, n_seq, out_dtype):
    m, d = u.shape
    taps = w_dw.shape[0]
    assert taps - 1 <= CONF_HALO and d % CONF_COLS == 0
    t = m // n_seq
    tr = min(t, 256)
    assert t % tr == 0 and tr % SUBLANES == 0
    tps = t // tr
    prev = halo is None
    if prev:
        assert tr % CONF_HALO == 0
        per = tr // CONF_HALO
        halo_arr = u
        halo_spec = pl.BlockSpec((CONF_HALO, d), lambda i: (jnp.maximum(i * per - 1, 0), 0))
    else:
        assert tps == 1
        halo_arr = halo
        halo_spec = pl.BlockSpec((CONF_HALO, d), lambda i: (i, 0))
    vmem = (2 * _nbytes((tr, d), F32) + 2 * _nbytes((CONF_HALO, d), F32) + 2 * _nbytes((tr, d), out_dtype)
            + _nbytes((2 * tr + CONF_HALO, d), F32) + 4 * _nbytes((32, d), F32))
    return pl.pallas_call(
        functools.partial(_conf_kernel, taps=taps, tiles_per_seq=tps, halo_is_prev_rows=prev),
        grid=(m // tr,),
        in_specs=[pl.BlockSpec((tr, d), lambda i: (i, 0)),
                  halo_spec,
                  pl.BlockSpec((taps, d), lambda i: (0, 0)),
                  pl.BlockSpec((1, d), lambda i: (0, 0)),
                  pl.BlockSpec((1, d), lambda i: (0, 0))],
        out_specs=pl.BlockSpec((tr, d), lambda i: (i, 0)),
        out_shape=jax.ShapeDtypeStruct((m, d), out_dtype),
        scratch_shapes=[pltpu.VMEM((CONF_HALO + tr, d), F32), pltpu.VMEM((tr, d), F32),
                        pltpu.VMEM((SUBLANES, tr + CONF_HALO, CONF_COLS), F32)],
        compiler_params=_params(("arbitrary",), vmem),
        name="conformer_conv",
    )(u, halo_arr, w_dw, ln_g.reshape(1, d), ln_b.reshape(1, d))


def _lane_cumsum(x, seg=None):
    n = x.shape[-1]
    lane = lax.broadcasted_iota(jnp.int32, x.shape, x.ndim - 1)
    pos = lane if seg is None else lane % seg
    d = 1
    while d < (n if seg is None else seg):
        x = x + jnp.where(pos >= d, pltpu.roll(x, d, x.ndim - 1), 0.0)
        d *= 2
    return x


def _logf_kernel(gt_ref, bf_ref, lf_ref, c_ref):
    lf = jax.nn.log_sigmoid(gt_ref[0] + bf_ref[...])
    lf_ref[0] = lf
    c_ref[0] = _lane_cumsum(lf)


def fox_logf(gates_t, b_f):
    n_seq, h, t = gates_t.shape
    spec = pl.BlockSpec((1, h, t), lambda s: (s, 0, 0))
    shape = jax.ShapeDtypeStruct((n_seq, h, t), F32)
    return pl.pallas_call(
        _logf_kernel,
        grid=(n_seq,),
        in_specs=[spec, pl.BlockSpec((h, 1), lambda s: (0, 0))],
        out_specs=[spec, spec],
        out_shape=[shape, shape],
        compiler_params=_params(("parallel",), 1 << 20),
        name="fox_logf",
    )(gates_t, b_f.reshape(h, 1))


def _flash_step(q, k_ref, v_ref, ck_ref, cq, carry, j, *, tk, scale, q0, masked):
    m, l, acc = carry
    ks = pl.multiple_of(j * tk, tk)
    k = k_ref[pl.ds(ks, tk), :].astype(BF16)
    v = v_ref[pl.ds(ks, tk), :].astype(BF16)
    s = _dot_nt(q, k) * scale
    s = s + (cq - ck_ref[0, 0, :, pl.ds(ks, tk)])
    if masked:
        qpos = q0 + lax.broadcasted_iota(jnp.int32, s.shape, 0)
        kpos = ks + lax.broadcasted_iota(jnp.int32, s.shape, 1)
        s = jnp.where(kpos <= qpos, s, -jnp.inf)
    m_new = jnp.maximum(m, jnp.max(s, axis=-1, keepdims=True))
    alpha = jnp.exp(m - m_new)
    p = jnp.exp(s - m_new)
    l = alpha * l + jnp.sum(p, axis=-1, keepdims=True)
    acc = alpha * acc + _dot(p.astype(BF16), v)
    return m_new, l, acc


def _flash_kernel(q_ref, k_ref, v_ref, cq_ref, ck_ref, o_ref, *, scale):
    tq = q_ref.shape[0]
    qi = pl.program_id(2)
    q = q_ref[...].astype(BF16)
    cq = cq_ref[0, 0]
    step = functools.partial(_flash_step, q, k_ref, v_ref, ck_ref, cq, tk=tq, scale=scale, q0=qi * tq)
    init = (jnp.full((tq, 1), -jnp.inf, F32), jnp.zeros((tq, 1), F32), jnp.zeros((tq, HEAD), F32))
    carry = lax.fori_loop(0, qi, lambda j, c: step(c, j, masked=False), init)
    m, l, acc = step(carry, qi, masked=True)
    o_ref[...] = (acc / l).astype(o_ref.dtype)


def fox_prompt_attention(q, k, v, c, *, n_seq):
    m, width = q.shape
    h = width // HEAD
    t = m // n_seq
    tq = min(t, 512)
    nq = t // tq
    cq = c.reshape(n_seq, h, t, 1)
    ck = c.reshape(n_seq, h, 1, t)
    vmem = (2 * _nbytes((tq, HEAD), F32) * 2 + 4 * _nbytes((t, HEAD), F32) + 2 * _nbytes((tq, LANES), F32)
            + 8 * _nbytes((tq, tq), F32))
    return pl.pallas_call(
        functools.partial(_flash_kernel, scale=HEAD ** -0.5),
        grid=(n_seq, h, nq),
        in_specs=[pl.BlockSpec((tq, HEAD), lambda b, hh, i: (b * nq + i, hh)),
                  pl.BlockSpec((t, HEAD), lambda b, hh, i: (b, hh)),
                  pl.BlockSpec((t, HEAD), lambda b, hh, i: (b, hh)),
                  pl.BlockSpec((1, 1, tq, 1), lambda b, hh, i: (b, hh, i, 0)),
                  pl.BlockSpec((1, 1, 1, t), lambda b, hh, i: (b, hh, 0, 0))],
        out_specs=pl.BlockSpec((tq, HEAD), lambda b, hh, i: (b * nq + i, hh)),
        out_shape=jax.ShapeDtypeStruct((m, width), BF16),
        compiler_params=_params(("parallel", "parallel", "arbitrary"), vmem),
        name="fox_prompt_attention",
    )(q, k, v, cq, ck)


def _split3(x):
    hi = x.astype(BF16)
    r = x - hi.astype(F32)
    mid = r.astype(BF16)
    lo = (r - mid.astype(F32)).astype(BF16)
    return hi, mid, lo


def _decode_attend(kget, vget, past_t, causal, qs_ref, cn_ref, s_ref, rep_ref, pv_ref, m_ref, l_ref, acc_ref, *,
                   heads, t, scale):
    keys = past_t.shape[1]
    for hh in range(heads):
        rows = slice(hh * t, (hh + 1) * t)
        s_ref[rows, :] = _dot_nt(qs_ref[rows, :].astype(BF16), kget(hh))
        rep_ref[rows, :] = jnp.broadcast_to(past_t[hh:hh + 1, :], (t, keys))
    s = s_ref[...] * scale + (cn_ref[0] - rep_ref[...])
    if causal:
        qpos = lax.broadcasted_iota(jnp.int32, s.shape, 0) % t
        kpos = lax.broadcasted_iota(jnp.int32, s.shape, 1)
        s = jnp.where(kpos <= qpos, s, -jnp.inf)
    m = m_ref[...]
    m_new = jnp.maximum(m, jnp.max(s, axis=-1, keepdims=True))
    alpha = jnp.exp(m - m_new)
    p = jnp.exp(s - m_new)
    l_ref[...] = alpha * l_ref[...] + jnp.sum(p, axis=-1, keepdims=True)
    m_ref[...] = m_new
    s_ref[...] = p
    for hh in range(heads):
        rows = slice(hh * t, (hh + 1) * t)
        pv_ref[rows, :] = _dot(s_ref[rows, :].astype(BF16), vget(hh))
    acc_ref[...] = alpha * acc_ref[...] + pv_ref[...]


def _decode_kernel(pt_ref, q_ref, kp_ref, vp_ref, lf_ref, kn_ref, vn_ref, cn_ref, cnrow_ref, o_ref,
                   qs_ref, s_ref, rep_ref, pv_ref, m_ref, l_ref, acc_ref, carry_ref, *, heads, t, scale):
    del pt_ref
    p = pl.program_id(1)

    @pl.when(p == 0)
    def _():
        m_ref[...] = jnp.full(m_ref.shape, -jnp.inf, F32)
        l_ref[...] = jnp.zeros(l_ref.shape, F32)
        acc_ref[...] = jnp.zeros(acc_ref.shape, F32)
        carry_ref[...] = jnp.zeros(carry_ref.shape, F32)
        for hh in range(heads):
            qs_ref[hh * t:(hh + 1) * t, :] = q_ref[:, hh * HEAD:(hh + 1) * HEAD]

    attend = functools.partial(_decode_attend, qs_ref=qs_ref, cn_ref=cn_ref, s_ref=s_ref, rep_ref=rep_ref,
                               pv_ref=pv_ref, m_ref=m_ref, l_ref=l_ref, acc_ref=acc_ref,
                               heads=heads, t=t, scale=scale)

    x = lf_ref[...]
    n = x.shape[0]
    row = lax.broadcasted_iota(jnp.int32, (n, n), 0)
    col = lax.broadcasted_iota(jnp.int32, (n, n), 1)
    after = jnp.where(row > col, 1.0, 0.0).astype(BF16)
    ones = jnp.ones((n, n), BF16)
    hi, mid, lo = _split3(x)
    later_t = (_dot_tn(hi, after) + _dot_tn(mid, after)) + _dot_tn(lo, after)
    past_t = -(later_t + carry_ref[...])
    carry_ref[...] = carry_ref[...] + ((_dot_tn(hi, ones) + _dot_tn(mid, ones)) + _dot_tn(lo, ones))
    attend(lambda hh: kp_ref[pl.ds(hh, n, stride=heads), :].astype(BF16),
           lambda hh: vp_ref[pl.ds(hh, n, stride=heads), :].astype(BF16), past_t, False)

    @pl.when(p == pl.num_programs(1) - 1)
    def _():
        attend(lambda hh: kn_ref[:, hh * HEAD:(hh + 1) * HEAD].astype(BF16),
               lambda hh: vn_ref[:, hh * HEAD:(hh + 1) * HEAD].astype(BF16), cnrow_ref[...], True)
        for hh in range(heads):
            rows = slice(hh * t, (hh + 1) * t)
            o_ref[:, hh * HEAD:(hh + 1) * HEAD] = acc_ref[rows, :] / l_ref[rows, :]


def fox_sample_attention(q, k_new, v_new, c_new, cache_k, cache_v, cache_logf, slot, page_table):
    n_seq, n_pages = page_table.shape
    m, width = q.shape
    h = width // HEAD
    t = m // n_seq
    slots, pool, page = cache_k.shape[:3]
    cache_k = cache_k.reshape(slots, pool, page * h, HEAD)
    cache_v = cache_v.reshape(slots, pool, page * h, HEAD)
    pad = lambda a: jnp.pad(a.reshape(n_seq, t, width), ((0, 0), (0, page - t), (0, 0)))
    cn_col = c_new[:, :, :t].reshape(n_seq, h * t, 1)
    cn_row = jnp.pad(c_new[:, :, :t], ((0, 0), (0, 0), (0, page - t)))
    last = n_pages - 1
    grid_spec = pltpu.PrefetchScalarGridSpec(
        num_scalar_prefetch=1,
        grid=(n_seq, n_pages),
        in_specs=[pl.BlockSpec((t, width), lambda b, p, pt: (b, 0)),
                  pl.BlockSpec((None, None, page * h, HEAD), lambda b, p, pt: (slot, pt[b, last - p], 0, 0)),
                  pl.BlockSpec((None, None, page * h, HEAD), lambda b, p, pt: (slot, pt[b, last - p], 0, 0)),
                  pl.BlockSpec((None, None, page, h), lambda b, p, pt: (slot, pt[b, last - p], 0, 0)),
                  pl.BlockSpec((None, page, width), lambda b, p, pt: (b, 0, 0)),
                  pl.BlockSpec((None, page, width), lambda b, p, pt: (b, 0, 0)),
                  pl.BlockSpec((1, h * t, 1), lambda b, p, pt: (b, 0, 0)),
                  pl.BlockSpec((None, h, page), lambda b, p, pt: (b, 0, 0))],
        out_specs=pl.BlockSpec((t, width), lambda b, p, pt: (b, 0)),
        scratch_shapes=[pltpu.VMEM((h * t, HEAD), F32), pltpu.VMEM((h * t, page), F32),
                        pltpu.VMEM((h * t, page), F32), pltpu.VMEM((h * t, HEAD), F32),
                        pltpu.VMEM((h * t, 1), F32), pltpu.VMEM((h * t, 1), F32),
                        pltpu.VMEM((h * t, HEAD), F32), pltpu.VMEM((h, page), F32)],
    )
    vmem = 8 * _nbytes((page, width), F32) + 2 * _nbytes((page, width), BF16)
    return pl.pallas_call(
        functools.partial(_decode_kernel, heads=h, t=t, scale=HEAD ** -0.5),
        grid_spec=grid_spec,
        out_shape=jax.ShapeDtypeStruct((m, width), F32),
        compiler_params=_params(("arbitrary", "arbitrary"), vmem),
        name="fox_sample_attention",
    )(page_table, q, cache_k, cache_v, cache_logf, pad(k_new), pad(v_new), cn_col, cn_row)


def _gdn_gate_kernel(gt_ref, alog_ref, dtb_ref, beta_ref, gc_ref, *, heads, t_valid):
    g_all = gt_ref[0]
    lane = lax.broadcasted_iota(jnp.int32, (heads, g_all.shape[1]), 1)
    valid = lane < t_valid
    beta = jnp.where(valid, jax.nn.sigmoid(g_all[:heads]), 0.0)
    g = -jnp.exp(alog_ref[...]) * jax.nn.softplus(g_all[heads:] + dtb_ref[...])
    g = jnp.where(valid, g, 0.0)
    beta_ref[0] = beta
    gc_ref[0] = _lane_cumsum(g, GDN_CHUNK)


def gdn_gates(gates_t, a_log, dt_bias, *, t_valid):
    n_seq, h2, t = gates_t.shape
    h = h2 // 2
    out_spec = pl.BlockSpec((1, h, t), lambda s: (s, 0, 0))
    shape = jax.ShapeDtypeStruct((n_seq, h, t), F32)
    return pl.pallas_call(
        functools.partial(_gdn_gate_kernel, heads=h, t_valid=t_valid),
        grid=(n_seq,),
        in_specs=[pl.BlockSpec((1, h2, t), lambda s: (s, 0, 0)),
                  pl.BlockSpec((h, 1), lambda s: (0, 0)),
                  pl.BlockSpec((h, 1), lambda s: (0, 0))],
        out_specs=[out_spec, out_spec],
        out_shape=[shape, shape],
        compiler_params=_params(("parallel",), 1 << 20),
        name="gdn_gates",
    )(gates_t, a_log.reshape(h, 1), dt_bias.reshape(h, 1))


def _row_to_col(row, eye):
    return jnp.sum(jnp.where(eye, row, 0.0), axis=1, keepdims=True)


def _gdn_kernel(q_ref, k_ref, v_ref, z_ref, beta_ref, gc_ref, s0_ref, ng_ref, o_ref, s_ref,
                u_s, wq_s, kd_s, in_s, *, hb, n_chunks):
    c_len = GDN_CHUNK
    n = hb * c_len
    row = lax.broadcasted_iota(jnp.int32, (n, n), 0)
    col = lax.broadcasted_iota(jnp.int32, (n, n), 1)
    eye = row == col
    same_head = (row // c_len) == (col // c_len)
    lower = jnp.logical_and(same_head, col <= row)
    strict = jnp.logical_and(same_head, col < row)
    ng = ng_ref[...]
    heads = range(hb)
    stack = lambda f: jnp.concatenate([f(hh) for hh in heads], axis=0)
    blk = lambda a, hh: a[hh * c_len:(hh + 1) * c_len, hh * HEAD:(hh + 1) * HEAD]

    def last_of(gc_row, hh):
        return gc_row[:, (hh + 1) * c_len - 1:(hh + 1) * c_len]

    def prepare(c, carry):
        r0 = pl.multiple_of(c * c_len, c_len)

        def unit(ref, hh, scale):
            a = ref[pl.ds(r0, c_len), hh * HEAD:(hh + 1) * HEAD]
            return a * lax.rsqrt(jnp.sum(a * a, axis=-1, keepdims=True) + EPS) * scale

        qn = stack(lambda hh: unit(q_ref, hh, HEAD ** -0.5))
        kn = stack(lambda hh: unit(k_ref, hh, 1.0))
        v = stack(lambda hh: v_ref[pl.ds(r0, c_len), hh * HEAD:(hh + 1) * HEAD])
        beta_row = beta_ref[0, 0, pl.ds(c, 1), :]
        gc_row = gc_ref[0, 0, pl.ds(c, 1), :]
        beta_col = _row_to_col(beta_row, eye)
        gc_col = _row_to_col(gc_row, eye)
        gc_last_col = stack(lambda hh: jnp.broadcast_to(last_of(gc_row, hh), (c_len, 1)))
        decay = jnp.exp(jnp.where(lower, gc_col - gc_row, -jnp.inf))
        kn16 = kn.astype(BF16)
        lmat = jnp.where(strict, _dot_nt(kn16, kn16) * decay * beta_col, 0.0)
        inv = jnp.where(eye, 1.0, 0.0) - lmat
        power = lmat
        span = 2
        while span < c_len:
            p16 = power.astype(BF16)
            power = _dot(p16, p16)
            inv = inv + _dot(inv.astype(BF16), power.astype(BF16))
            span *= 2
        e_col = jnp.exp(gc_col)
        rhs = jnp.concatenate([v * beta_col, kn * (beta_col * e_col)], axis=1)
        sol = _dot(inv.astype(BF16), rhs.astype(BF16))
        u_s[c] = sol[:, :HEAD]
        wq_s[c, 0:n, :] = sol[:, HEAD:].astype(BF16)
        wq_s[c, n:2 * n, :] = (qn * e_col).astype(BF16)
        in_s[c] = (_dot_nt(qn.astype(BF16), kn16) * decay).astype(BF16)
        kd_s[c] = (kn * jnp.exp(gc_last_col - gc_col)).astype(BF16)
        return carry

    lax.fori_loop(0, n_chunks, prepare, 0, unroll=2 if n_chunks % 2 == 0 else 1)

    head_of_row = lax.broadcasted_iota(jnp.int32, (n, HEAD), 0) // c_len

    def advance(c, s_wide):
        r0 = pl.multiple_of(c * c_len, c_len)
        r = _dot(wq_s[c], s_wide.astype(BF16))
        w_s = stack(lambda hh: blk(r, hh))
        q_s = stack(lambda hh: blk(r[n:], hh))
        v_new = u_s[c] - w_s
        o = q_s + _dot(in_s[c], v_new.astype(BF16))
        v_wide = jnp.concatenate([jnp.where(head_of_row == hh, v_new, 0.0) for hh in heads], axis=1).astype(BF16)
        gc_row = gc_ref[0, 0, pl.ds(c, 1), :]
        g_last = jnp.concatenate([jnp.broadcast_to(jnp.exp(last_of(gc_row, hh)), (1, HEAD)) for hh in heads], axis=1)
        s_next = s_wide * g_last + _dot_tn(kd_s[c], v_wide)
        for hh in heads:
            cols = slice(hh * HEAD, (hh + 1) * HEAD)
            oh = o[hh * c_len:(hh + 1) * c_len]
            on = oh * lax.rsqrt(jnp.mean(oh * oh, axis=-1, keepdims=True) + EPS) * ng
            z = z_ref[pl.ds(r0, c_len), cols]
            o_ref[pl.ds(r0, c_len), cols] = (on * _silu(z)).astype(o_ref.dtype)
        return s_next

    s_wide = lax.fori_loop(0, n_chunks, advance, jnp.concatenate([s0_ref[0, hh] for hh in heads], axis=1))
    for hh in heads:
        s_ref[0, hh] = s_wide[:, hh * HEAD:(hh + 1) * HEAD]


def gdn_mix(qkv, z, z_col_off, beta, gc, s0, norm_g, *, n_seq, out_dtype):
    m = qkv.shape[0]
    h = qkv.shape[1] // (3 * HEAD)
    t = m // n_seq
    n_chunks = t // GDN_CHUNK
    hb = 4
    assert h % hb == 0
    nhb = h // hb
    wblk = hb * HEAD
    zoff = z_col_off // wblk
    n = hb * GDN_CHUNK
    by_block = lambda a: jnp.transpose(a.reshape(n_seq, nhb, hb, n_chunks, GDN_CHUNK), (0, 1, 3, 2, 4)).reshape(
        n_seq, nhb, n_chunks, n)
    beta, gc = by_block(beta), by_block(gc)
    seq_spec = lambda off: pl.BlockSpec((t, wblk), lambda b, j: (b, j + off))
    gate_spec = pl.BlockSpec((1, 1, n_chunks, n), lambda b, j: (b, j, 0, 0))
    state_spec = pl.BlockSpec((1, hb, HEAD, HEAD), lambda b, j: (b, j, 0, 0))
    scratch = [pltpu.VMEM((n_chunks, n, HEAD), F32), pltpu.VMEM((n_chunks, 2 * n, HEAD), BF16),
               pltpu.VMEM((n_chunks, n, HEAD), BF16), pltpu.VMEM((n_chunks, n, n), BF16)]
    vmem = (2 * 4 * _nbytes((t, wblk), F32) + 2 * _nbytes((t, wblk), out_dtype)
            + _nbytes((n_chunks, n, HEAD), F32) + _nbytes((n_chunks, 3 * n + 2 * n, HEAD), BF16))
    return pl.pallas_call(
        functools.partial(_gdn_kernel, hb=hb, n_chunks=n_chunks),
        grid=(n_seq, nhb),
        in_specs=[seq_spec(0), seq_spec(nhb), seq_spec(2 * nhb), seq_spec(zoff),
                  gate_spec, gate_spec, state_spec,
                  pl.BlockSpec((1, HEAD), lambda b, j: (0, 0))],
        out_specs=[pl.BlockSpec((t, wblk), lambda b, j: (b, j)), state_spec],
        out_shape=[jax.ShapeDtypeStruct((m, h * HEAD), out_dtype),
                   jax.ShapeDtypeStruct((n_seq, h, HEAD, HEAD), F32)],
        scratch_shapes=scratch,
        compiler_params=_params(("parallel", "parallel"), vmem),
        name="gdn_mix",
    )(qkv, qkv, qkv, z, beta, gc, s0, norm_g.reshape(1, HEAD))


def _xattn_q_kernel(x_ref, w_ref, mk_ref, mv_ref, o_ref, wb_ref, *, heads, scale):
    @pl.when(pl.program_id(0) == 0)
    def _():
        wb_ref[...] = w_ref[...].astype(BF16)

    q = _dot(x_ref[...].astype(BF16), wb_ref[...])
    for hh in range(heads):
        cols = slice(hh * HEAD, (hh + 1) * HEAD)
        s = _dot_nt(q[:, cols].astype(BF16), mk_ref[0, :, cols].astype(BF16)) * scale
        p = jnp.exp(s - jnp.max(s, axis=-1, keepdims=True))
        o = _dot(p.astype(BF16), mv_ref[0, :, cols].astype(BF16)) / jnp.sum(p, axis=-1, keepdims=True)
        o_ref[:, cols] = o.astype(o_ref.dtype)


def xattn_heads(x, w_q, mem_k, mem_v, layer, *, n_seq):
    m, d = x.shape
    width = w_q.shape[2]
    t = m // n_seq
    tm = min(t, _row_tile(m, x.dtype, d))
    tps = t // tm
    mem = mem_k.shape[2]
    vmem = (2 * _nbytes((tm, d), x.dtype) + 3 * _nbytes((d, width), F32) + 4 * _nbytes((mem, width), F32)
            + 2 * _nbytes((tm, width), F32) + 8 * _nbytes((tm, mem), F32))
    return pl.pallas_call(
        functools.partial(_xattn_q_kernel, heads=width // HEAD, scale=HEAD ** -0.5),
        grid=(m // tm,),
        in_specs=[pl.BlockSpec((tm, d), lambda i: (i, 0)),
                  pl.BlockSpec((None, d, width), lambda i: (layer, 0, 0)),
                  pl.BlockSpec((None, 1, mem, width), lambda i: (layer, i // tps, 0, 0)),
                  pl.BlockSpec((None, 1, mem, width), lambda i: (layer, i // tps, 0, 0))],
        out_specs=pl.BlockSpec((tm, width), lambda i: (i, 0)),
        out_shape=jax.ShapeDtypeStruct((m, width), F32),
        scratch_shapes=[pltpu.VMEM((d, width), BF16)],
        compiler_params=_params(("arbitrary",), vmem),
        name="xattn_heads",
    )(x, w_q, mem_k, mem_v)


def _xattn_o_kernel(a_ref, w_ref, res_ref, g_ref, x_ref, h_ref, wb_ref):
    @pl.when(pl.program_id(0) == 0)
    def _():
        wb_ref[...] = w_ref[...].astype(BF16)

    x = res_ref[...] + _dot(a_ref[...].astype(BF16), wb_ref[...])
    x_ref[...] = x
    h_ref[...] = _rmsnorm_rows(x, g_ref[...]).astype(h_ref.dtype)


def xattn_out(a, w_o, layer, res, g, h_dtype):
    m, width = a.shape
    d = w_o.shape[2]
    tm = min(m, 256)
    vmem = (2 * _nbytes((tm, width), F32) + 3 * _nbytes((width, d), F32) + 4 * _nbytes((tm, d), F32)
            + 2 * _nbytes((tm, d), h_dtype) + 2 * _nbytes((tm, d), F32))
    return pl.pallas_call(
        _xattn_o_kernel,
        grid=(m // tm,),
        in_specs=[pl.BlockSpec((tm, width), lambda i: (i, 0)),
                  pl.BlockSpec((None, width, d), lambda i: (layer, 0, 0)),
                  pl.BlockSpec((tm, d), lambda i: (i, 0)),
                  pl.BlockSpec((1, d), lambda i: (0, 0))],
        out_specs=[pl.BlockSpec((tm, d), lambda i: (i, 0)), pl.BlockSpec((tm, d), lambda i: (i, 0))],
        out_shape=[jax.ShapeDtypeStruct((m, d), F32), jax.ShapeDtypeStruct((m, d), h_dtype)],
        scratch_shapes=[pltpu.VMEM((width, d), BF16)],
        compiler_params=_params(("arbitrary",), vmem),
        name="xattn_out",
    )(a, w_o, res, g.reshape(1, d))


def _pad_lanes(a, mult):
    pad = (-a.shape[-1]) % mult
    return a if pad == 0 else jnp.pad(a, [(0, 0)] * (a.ndim - 1) + [(0, pad)])


def _trunk(x, n_seq, mem_k, mem_v, states, W):
    m, d = x.shape
    t = m // n_seq
    fresh = states is None
    act = BF16 if fresh else F32
    depth = W['norm_mix'].shape[0]
    ffn_dim = W['ffn_w_dw'].shape[2]
    heads = d // HEAD
    new_conf, new_k, new_v, new_lf, new_s, new_gbuf, new_ffn = [], [], [], [], [], [], []
    for i in range(depth):
        kind, slot = i % 3, i // 3
        h = rmsnorm(x, W['norm_mix'][i], act)
        if kind == 0:
            u = matmul_glu(h, W['conf_w_pw1'], slot)
            taps = W['conf_w_dw'].shape[1]
            if fresh:
                halo = None
                new_conf.append(u.reshape(n_seq, t, d)[:, t - (taps - 1):])
            else:
                buf = states['conf'][slot]
                halo = jnp.pad(buf, ((0, 0), (CONF_HALO - (taps - 1), 0), (0, 0))).reshape(n_seq * CONF_HALO, d)
                new_conf.append(_next_state(buf, u, n_seq, d))
            y = conformer_conv(u, halo, W['conf_w_dw'][slot], W['conf_ln_g'][slot], W['conf_ln_b'][slot],
                               n_seq=n_seq, out_dtype=act)
            x = matmul(y, W['conf_w_pw2'], slot, res=x)
        elif kind == 1:
            w_in = W['fox_w_in']
            fw = heads * HEAD
            q = matmul(h, w_in, slot, col_off=0, n_out=fw)
            k = matmul(h, w_in, slot, col_off=fw, n_out=fw)
            v = matmul(h, w_in, slot, col_off=2 * fw, n_out=fw)
            gate = matmul(h, w_in[slot, :, 3 * fw:])
            gates_t = _pad_lanes(jnp.transpose(gate.reshape(n_seq, t, heads), (0, 2, 1)), LANES)
            lf_t, c_t = fox_logf(gates_t, W['fox_b_f'][slot])
            if fresh:
                o = fox_prompt_attention(q, k, v, c_t, n_seq=n_seq)
            else:
                cache_k, cache_v, cache_lf, page_table = states['fox']
                o = fox_sample_attention(q, k, v, c_t, cache_k, cache_v, cache_lf, slot, page_table)
            x = matmul(o, W['fox_w_o'], slot, res=x)
            new_k.append(k.reshape(n_seq, t, heads, HEAD))
            new_v.append(v.reshape(n_seq, t, heads, HEAD))
            new_lf.append(jnp.transpose(lf_t[:, :, :t], (0, 2, 1)))
        else:
            w_in = W['gdn_w_in']
            cw = W['gdn_w_conv']
            n_qkv = cw.shape[2]
            hv = heads * HEAD
            if fresh:
                qkv, tails = matmul_conv(h, w_in, cw, slot, n_seq=n_seq, n_out=n_qkv, gated=False, out_dtype=F32)
                new_gbuf.append(tails[:, SUBLANES - (cw.shape[1] - 1):])
                zsrc, zoff = matmul(h, w_in, slot, col_off=n_qkv, n_out=hv), 0
                s0 = jnp.zeros((n_seq, heads, HEAD, HEAD), F32)
                t_pad = t
            else:
                pre = matmul(h, w_in, slot, col_off=0, n_out=n_qkv + hv)
                buf = states['gdn_conv'][slot]
                qkv = state_conv(pre, buf, cw[slot], n_out=n_qkv, gated=False)
                new_gbuf.append(_next_state(buf, pre, n_seq, n_qkv))
                t_pad = -(-t // GDN_CHUNK) * GDN_CHUNK
                rows = lambda a: jnp.pad(a.reshape(n_seq, t, -1), ((0, 0), (0, t_pad - t), (0, 0))).reshape(
                    n_seq * t_pad, -1)
                qkv = rows(qkv)
                zsrc, zoff = rows(pre[:, n_qkv:]), 0
                s0 = states['gdn'][slot]
            gate = matmul(h, w_in[slot, :, n_qkv + hv:])
            gates_t = jnp.transpose(gate.reshape(n_seq, t, 2 * heads), (0, 2, 1))
            gates_t = jnp.pad(gates_t, ((0, 0), (0, 0), (0, max(t_pad, LANES) - t)))
            beta, gc = gdn_gates(gates_t, W['gdn_a_log'][slot], W['gdn_dt_bias'][slot], t_valid=t)
            o, s_new = gdn_mix(qkv, zsrc, zoff, beta[:, :, :t_pad], gc[:, :, :t_pad], s0, W['gdn_norm_g'][slot],
                               n_seq=n_seq, out_dtype=act)
            if t_pad != t:
                o = o.reshape(n_seq, t_pad, hv)[:, :t].reshape(m, hv)
            x = matmul(o, W['gdn_w_o'], slot, res=x)
            new_s.append(s_new)
        h = rmsnorm(x, W['norm_mem'][i], act)
        a = xattn_heads(h, W['x_w_q'], mem_k, mem_v, i, n_seq=n_seq)
        x, h = xattn_out(a, W['x_w_o'], i, x, W['norm_ffn'][i], act)
        w_up = W['ffn_w_up']
        cw = W['ffn_w_dw']
        if fresh:
            y, tails = matmul_conv(h, w_up, cw, i, n_seq=n_seq, n_out=ffn_dim, gated=True, out_dtype=act)
            new_ffn.append(tails[:, SUBLANES - (cw.shape[1] - 1):])
        else:
            pre = matmul(h, w_up, i)
            buf = states['ffn_conv'][i]
            y = state_conv(pre, buf, cw[i], n_out=ffn_dim, gated=True)
            new_ffn.append(_next_state(buf, pre, n_seq, ffn_dim))
        x = matmul(y, W['ffn_w_down'], i, res=x)
    out = rmsnorm(x, W['norm_final'], F32)
    return (out.reshape(n_seq, t, d), jnp.stack(new_conf), jnp.stack(new_k), jnp.stack(new_v), jnp.stack(new_lf),
            jnp.stack(new_s), jnp.stack(new_gbuf), jnp.stack(new_ffn))


def kernel(x_prompt, x_sample, cache_fox_k, cache_fox_v, cache_fox_logf, cache_mem_k, cache_mem_v, state_conf,
           state_gdn, state_gdn_conv, state_ffn_conv, page_table, mem_prompt, norm_mix, norm_mem, norm_ffn,
           norm_final, conf_w_pw1, conf_w_dw, conf_ln_g, conf_ln_b, conf_w_pw2, fox_w_in, fox_b_f, fox_w_o,
           gdn_w_in, gdn_w_conv, gdn_a_log, gdn_dt_bias, gdn_norm_g, gdn_w_o, x_w_q, x_w_kv, x_w_o, ffn_w_up,
           ffn_w_dw, ffn_w_down):
    W = dict(norm_mix=norm_mix, norm_mem=norm_mem, norm_ffn=norm_ffn, norm_final=norm_final,
             conf_w_pw1=conf_w_pw1, conf_w_dw=conf_w_dw, conf_ln_g=conf_ln_g, conf_ln_b=conf_ln_b,
             conf_w_pw2=conf_w_pw2, fox_w_in=fox_w_in, fox_b_f=fox_b_f, fox_w_o=fox_w_o,
             gdn_w_in=gdn_w_in, gdn_w_conv=gdn_w_conv, gdn_a_log=gdn_a_log, gdn_dt_bias=gdn_dt_bias,
             gdn_norm_g=gdn_norm_g, gdn_w_o=gdn_w_o, x_w_q=x_w_q, x_w_o=x_w_o,
             ffn_w_up=ffn_w_up, ffn_w_dw=ffn_w_dw, ffn_w_down=ffn_w_down)
    b, t, d = x_prompt.shape
    bs, ts, _ = x_sample.shape
    depth = x_w_kv.shape[0]
    mem = mem_prompt.shape[1]
    xw = x_w_kv.shape[2] // 2
    xh = xw // HEAD

    mem_rows = mem_prompt.reshape(b * mem, d)
    kv = jnp.stack([matmul(mem_rows, x_w_kv, i) for i in range(depth)])
    mem_k_p = kv[..., :xw].reshape(depth, b, mem, xw)
    mem_v_p = kv[..., xw:].reshape(depth, b, mem, xw)

    (y_p, conf_p, fox_k_p, fox_v_p, fox_lf_p, gdn_p, gdn_conv_p, ffn_conv_p) = _trunk(
        x_prompt.reshape(b * t, d), b, mem_k_p, mem_v_p, None, W)

    states = dict(conf=state_conf, gdn=state_gdn, gdn_conv=state_gdn_conv, ffn_conv=state_ffn_conv,
                  fox=(cache_fox_k, cache_fox_v, cache_fox_logf, page_table))
    (y_s, conf_s, fox_k_s, fox_v_s, fox_lf_s, gdn_s, gdn_conv_s, ffn_conv_s) = _trunk(
        x_sample.reshape(bs * ts, d), bs, cache_mem_k.reshape(depth, bs, mem, xw),
        cache_mem_v.reshape(depth, bs, mem, xw), states, W)

    return (y_p, y_s,
            conf_p, fox_k_p, fox_v_p, fox_lf_p, gdn_p, gdn_conv_p,
            mem_k_p.reshape(depth, b, mem, xh, HEAD), mem_v_p.reshape(depth, b, mem, xh, HEAD), ffn_conv_p,
            conf_s, fox_k_s, fox_v_s, fox_lf_s, gdn_s, gdn_conv_s, ffn_conv_s)
```

```python
import functools

import jax
import jax.numpy as jnp
from jax import lax
from jax.experimental import pallas as pl
from jax.experimental.pallas import tpu as pltpu

EPS = 1e-6
HEAD = 128
GDN_CHUNK = 64
SUBLANES = 8
LANES = 128
VMEM_CAP = 58 * 1024 * 1024
BF16 = jnp.bfloat16
F32 = jnp.float32


def _params(semantics, vmem_bytes):
    limit = int(min(VMEM_CAP, max(vmem_bytes * 5 // 4 + (4 << 20), 16 << 20)))
    return pltpu.CompilerParams(dimension_semantics=semantics, vmem_limit_bytes=limit)


def _nbytes(shape, dtype):
    n = 1
    for s in shape:
        n *= s
    return n * jnp.dtype(dtype).itemsize


def _silu(x):
    return x * jax.nn.sigmoid(x)


def _dot(a, b):
    return jnp.dot(a, b, preferred_element_type=F32)


def _dot_nt(a, b):
    return lax.dot_general(a, b, (((1,), (1,)), ((), ())), preferred_element_type=F32)


def _dot_tn(a, b):
    return lax.dot_general(a, b, (((0,), (0,)), ((), ())), preferred_element_type=F32)


def _rmsnorm_rows(x, g):
    return x * lax.rsqrt(jnp.mean(x * x, axis=-1, keepdims=True) + EPS) * g


def _rmsnorm_kernel(x_ref, g_ref, o_ref):
    o_ref[...] = _rmsnorm_rows(x_ref[...], g_ref[...]).astype(o_ref.dtype)


def rmsnorm(x, g, out_dtype):
    m, d = x.shape
    tm = min(m, 256)
    vmem = 2 * (_nbytes((tm, d), F32) + _nbytes((tm, d), out_dtype))
    return pl.pallas_call(
        _rmsnorm_kernel,
        grid=(m // tm,),
        in_specs=[pl.BlockSpec((tm, d), lambda i: (i, 0)),
                  pl.BlockSpec((1, d), lambda i: (0, 0))],
        out_specs=pl.BlockSpec((tm, d), lambda i: (i, 0)),
        out_shape=jax.ShapeDtypeStruct((m, d), out_dtype),
        compiler_params=_params(("parallel",), vmem),
        name="rmsnorm",
    )(x, g.reshape(1, d))


def _row_tile(m, x_dtype, k):
    cap = 1024 if jnp.dtype(x_dtype).itemsize == 2 else 512
    if k > 4096:
        cap //= 4
    tm = min(m, cap)
    assert m % tm == 0
    return tm


def _col_tile(n, cap):
    tn = min(n, cap)
    assert n % tn == 0
    return tn


def _mm_kernel(*refs, has_res, has_x2):
    refs = list(refs)
    x_ref, w_ref = refs[:2]
    del refs[:2]
    res_ref = refs.pop(0) if has_res else None
    x2_ref = refs.pop(0) if has_x2 else None
    res2_ref = refs.pop(0) if has_x2 and has_res else None
    o_ref = refs.pop(0)
    o2_ref = refs.pop(0) if has_x2 else None
    wb_ref, = refs

    @pl.when(pl.program_id(1) == 0)
    def _():
        wb_ref[...] = w_ref[...].astype(BF16)
        if has_x2:
            acc2 = _dot(x2_ref[...].astype(BF16), wb_ref[...])
            if has_res:
                acc2 = res2_ref[...] + acc2
            o2_ref[...] = acc2

    acc = _dot(x_ref[...].astype(BF16), wb_ref[...])
    if has_res:
        acc = res_ref[...] + acc
    o_ref[...] = acc.astype(o_ref.dtype)


def _stacked(w, layer):
    return (w[None], 0) if w.ndim == 2 else (w, layer)


def _x2_spec(x2):
    return pl.BlockSpec(x2.shape, lambda n, i: (0, 0), pipeline_mode=pl.Buffered(1))


def matmul(x, w, layer=0, *, col_off=0, n_out=None, res=None, out_dtype=F32, x2=None, res2=None):
    w, layer = _stacked(w, layer)
    m, k = x.shape
    n_out = w.shape[2] - col_off if n_out is None else n_out
    deep = k > 4096
    tn = _col_tile(n_out, 512)
    assert col_off % tn == 0
    off = col_off // tn
    tm = _row_tile(m, x.dtype, k)
    w_bufs = 1 if deep else 2
    w_mode = dict(pipeline_mode=pl.Buffered(1)) if deep else {}
    tile = pl.BlockSpec((tm, tn), lambda n, i: (i, n))
    in_specs = [pl.BlockSpec((tm, k), lambda n, i: (i, 0)),
                pl.BlockSpec((None, k, tn), lambda n, i: (layer, 0, n + off), **w_mode)]
    args = [x, w]
    out_specs = [tile]
    out_shape = [jax.ShapeDtypeStruct((m, n_out), out_dtype)]
    vmem = (2 * _nbytes((tm, k), x.dtype) + w_bufs * _nbytes((k, tn), F32) + _nbytes((k, tn), BF16)
            + 2 * _nbytes((tm, tn), out_dtype))
    if res is not None:
        in_specs.append(tile)
        args.append(res)
        vmem += 2 * _nbytes((tm, tn), F32)
    if x2 is not None:
        m2 = x2.shape[0]
        tile2 = pl.BlockSpec((m2, tn), lambda n, i: (0, n))
        in_specs.append(_x2_spec(x2))
        args.append(x2)
        if res is not None:
            in_specs.append(tile2)
            args.append(res2)
        out_specs.append(tile2)
        out_shape.append(jax.ShapeDtypeStruct((m2, n_out), F32))
        vmem += _nbytes(x2.shape, x2.dtype) + 4 * _nbytes((m2, tn), F32)
    out = pl.pallas_call(
        functools.partial(_mm_kernel, has_res=res is not None, has_x2=x2 is not None),
        grid=(n_out // tn, m // tm),
        in_specs=in_specs,
        out_specs=out_specs,
        out_shape=out_shape,
        scratch_shapes=[pltpu.VMEM((k, tn), BF16)],
        compiler_params=_params(("arbitrary", "arbitrary"), vmem),
        name="matmul",
    )(*args)
    return out[0] if x2 is None else out


def _glu_kernel(x_ref, wa_ref, wb_ref, x2_ref, o_ref, o2_ref, wab_ref, wbb_ref):
    glu = lambda x: _dot(x, wab_ref[...]) * jax.nn.sigmoid(_dot(x, wbb_ref[...]))

    @pl.when(pl.program_id(1) == 0)
    def _():
        wab_ref[...] = wa_ref[...].astype(BF16)
        wbb_ref[...] = wb_ref[...].astype(BF16)
        o2_ref[...] = glu(x2_ref[...].astype(BF16))

    o_ref[...] = glu(x_ref[...].astype(BF16))


def matmul_glu(x, w, layer, x2):
    m, k = x.shape
    m2 = x2.shape[0]
    n = w.shape[2] // 2
    tn = _col_tile(n, 256)
    tm = _row_tile(m, x.dtype, k)
    nb = n // tn
    vmem = (2 * _nbytes((tm, k), x.dtype) + 4 * _nbytes((k, tn), F32) + 2 * _nbytes((k, tn), BF16)
            + 2 * _nbytes((tm, tn), F32) + _nbytes(x2.shape, x2.dtype) + 2 * _nbytes((m2, tn), F32))
    return pl.pallas_call(
        _glu_kernel,
        grid=(nb, m // tm),
        in_specs=[pl.BlockSpec((tm, k), lambda j, i: (i, 0)),
                  pl.BlockSpec((None, k, tn), lambda j, i: (layer, 0, j)),
                  pl.BlockSpec((None, k, tn), lambda j, i: (layer, 0, j + nb)),
                  _x2_spec(x2)],
        out_specs=[pl.BlockSpec((tm, tn), lambda j, i: (i, j)), pl.BlockSpec((m2, tn), lambda j, i: (0, j))],
        out_shape=[jax.ShapeDtypeStruct((m, n), F32), jax.ShapeDtypeStruct((m2, n), F32)],
        scratch_shapes=[pltpu.VMEM((k, tn), BF16), pltpu.VMEM((k, tn), BF16)],
        compiler_params=_params(("arbitrary", "arbitrary"), vmem),
        name="matmul_glu",
    )(x, w, w, x2)


def _mm_conv_kernel(*refs, width, tiles_per_seq, gated):
    if gated:
        x_ref, wg_ref, wv_ref, cw_ref, x2_ref, y_ref, tail_ref, g2_ref, v2_ref, wgb_ref, wvb_ref, ext_ref = refs
    else:
        x_ref, wg_ref, cw_ref, x2_ref, y_ref, tail_ref, g2_ref, wgb_ref, ext_ref = refs
    i = pl.program_id(1)

    @pl.when(i == 0)
    def _():
        x2 = x2_ref[...].astype(BF16)
        wgb_ref[...] = wg_ref[...].astype(BF16)
        g2_ref[...] = _dot(x2, wgb_ref[...])
        if gated:
            wvb_ref[...] = wv_ref[...].astype(BF16)
            v2_ref[...] = _dot(x2, wvb_ref[...])

    @pl.when(i % tiles_per_seq == 0)
    def _():
        ext_ref[0:SUBLANES, :] = jnp.zeros((SUBLANES, ext_ref.shape[1]), F32)

    x = x_ref[...].astype(BF16)
    g = _dot(x, wgb_ref[...])
    tm = g.shape[0]
    ext_ref[SUBLANES:SUBLANES + tm, :] = g
    cw = cw_ref[...]
    y = g * cw[width - 1:width, :]
    for j in range(width - 1):
        y = y + ext_ref[pl.ds(SUBLANES - (width - 1) + j, tm), :] * cw[j:j + 1, :]
    y = _silu(y)
    if gated:
        y = y * _dot(x, wvb_ref[...])
    y_ref[...] = y.astype(y_ref.dtype)
    tail = g[tm - SUBLANES:tm, :]
    tail_ref[0] = tail
    ext_ref[0:SUBLANES, :] = tail


def matmul_conv(x, w, cw, layer, x2, *, n_seq, n_out, gated, out_dtype):
    m, k = x.shape
    m2 = x2.shape[0]
    width = cw.shape[1]
    tn = _col_tile(n_out, 256 if gated else 512)
    t = m // n_seq
    tm = _row_tile(t, x.dtype, k)
    tps = t // tm
    nb = n_out // tn
    in_specs = [pl.BlockSpec((tm, k), lambda j, i: (i, 0)),
                pl.BlockSpec((None, k, tn), lambda j, i: (layer, 0, j))]
    args = [x, w]
    scratch = [pltpu.VMEM((k, tn), BF16)]
    if gated:
        in_specs.append(pl.BlockSpec((None, k, tn), lambda j, i: (layer, 0, j + nb)))
        args.append(w)
        scratch.append(pltpu.VMEM((k, tn), BF16))
    in_specs.append(pl.BlockSpec((None, width, tn), lambda j, i: (layer, 0, j)))
    args.append(cw)
    in_specs.append(_x2_spec(x2))
    args.append(x2)
    scratch.append(pltpu.VMEM((tm + SUBLANES, tn), F32))
    nw = 2 if gated else 1
    vmem = (2 * _nbytes((tm, k), x.dtype) + nw * (2 * _nbytes((k, tn), F32) + _nbytes((k, tn), BF16))
            + 2 * _nbytes((tm, tn), out_dtype) + 6 * _nbytes((tm, tn), F32)
            + _nbytes(x2.shape, x2.dtype) + 2 * nw * _nbytes((m2, tn), F32))
    tile2 = pl.BlockSpec((m2, tn), lambda j, i: (0, j))
    plain2 = jax.ShapeDtypeStruct((m2, n_out), F32)
    return pl.pallas_call(
        functools.partial(_mm_conv_kernel, width=width, tiles_per_seq=tps, gated=gated),
        grid=(nb, m // tm),
        in_specs=in_specs,
        out_specs=[pl.BlockSpec((tm, tn), lambda j, i: (i, j)),
                   pl.BlockSpec((1, SUBLANES, tn), lambda j, i: (i // tps, 0, j))] + [tile2] * nw,
        out_shape=[jax.ShapeDtypeStruct((m, n_out), out_dtype),
                   jax.ShapeDtypeStruct((n_seq, SUBLANES, n_out), F32)] + [plain2] * nw,
        scratch_shapes=scratch,
        compiler_params=_params(("arbitrary", "arbitrary"), vmem),
        name="matmul_conv",
    )(*args)


def _state_conv_kernel(*refs, width, gated):
    if gated:
        u_ref, v_ref, st_ref, cw_ref, y_ref, ext_ref = refs
    else:
        u_ref, st_ref, cw_ref, y_ref, ext_ref = refs
    t = u_ref.shape[0]
    ext_ref[SUBLANES - (width - 1):SUBLANES, :] = st_ref[0]
    ext_ref[SUBLANES:SUBLANES + t, :] = u_ref[...]
    cw = cw_ref[...]
    y = jnp.zeros(u_ref.shape, F32)
    for j in range(width):
        y = y + ext_ref[pl.ds(SUBLANES - (width - 1) + j, t), :] * cw[j:j + 1, :]
    y = _silu(y)
    if gated:
        y = y * v_ref[...]
    y_ref[...] = y


def state_conv(u, v, state, cw):
    n_seq, wm1, n_out = state.shape
    t = u.shape[0] // n_seq
    width = wm1 + 1
    gated = v is not None
    tc = _col_tile(n_out, 256)
    nb = n_out // tc
    in_specs = [pl.BlockSpec((t, tc), lambda s, j: (s, j))]
    args = [u]
    if gated:
        in_specs.append(pl.BlockSpec((t, tc), lambda s, j: (s, j)))
        args.append(v)
    in_specs += [pl.BlockSpec((1, wm1, tc), lambda s, j: (s, 0, j)),
                 pl.BlockSpec((width, tc), lambda s, j: (0, j))]
    args += [state, cw]
    return pl.pallas_call(
        functools.partial(_state_conv_kernel, width=width, gated=gated),
        grid=(n_seq, nb),
        in_specs=in_specs,
        out_specs=pl.BlockSpec((t, tc), lambda s, j: (s, j)),
        out_shape=jax.ShapeDtypeStruct((n_seq * t, n_out), F32),
        scratch_shapes=[pltpu.VMEM((SUBLANES + t, tc), F32)],
        compiler_params=_params(("parallel", "parallel"), 1 << 20),
        name="state_conv",
    )(*args)


def _next_state(state, u, n_seq, n_out):
    wm1 = state.shape[1]
    ext = jnp.concatenate([state, u[:, :n_out].reshape(n_seq, -1, n_out)], axis=1)
    return ext[:, ext.shape[1] - wm1:]


CONF_HALO = 32
CONF_ROWS = 64
CONF_COLS = 256


def _conf_kernel(u_ref, halo_ref, w_ref, lg_ref, lb_ref, o_ref, ext_ref, y_ref, sh_ref, *,
                 taps, tiles_per_seq, halo_is_prev_rows):
    tr, d = u_ref.shape
    if halo_is_prev_rows:
        first = pl.program_id(0) % tiles_per_seq == 0

        @pl.when(first)
        def _():
            ext_ref[0:CONF_HALO, :] = jnp.zeros((CONF_HALO, d), F32)

        @pl.when(jnp.logical_not(first))
        def _():
            ext_ref[0:CONF_HALO, :] = halo_ref[...]
    else:
        ext_ref[0:CONF_HALO, :] = halo_ref[...]
    ext_ref[CONF_HALO:CONF_HALO + tr, :] = u_ref[...]

    rows = min(tr, CONF_ROWS)
    lead = CONF_HALO - (taps - 1)

    def col_block(c, carry):
        c0 = pl.multiple_of(c * CONF_COLS, CONF_COLS)
        w = w_ref[:, pl.ds(c0, CONF_COLS)]
        for r in range(min(SUBLANES, taps)):
            span = tr + (len(range(r, taps, SUBLANES)) - 1) * SUBLANES
            sh_ref[r, 0:span, :] = ext_ref[pl.ds(lead + r, span), pl.ds(c0, CONF_COLS)]
        for r0 in range(0, tr, rows):
            acc = jnp.zeros((rows, CONF_COLS), F32)
            for j in range(taps):
                r = j % SUBLANES
                acc = acc + sh_ref[r, r0 + j - r:r0 + j - r + rows, :] * w[j:j + 1, :]
            y_ref[r0:r0 + rows, pl.ds(c0, CONF_COLS)] = acc
        return carry

    lax.fori_loop(0, d // CONF_COLS, col_block, 0)

    def row_block(i, carry):
        r0 = pl.multiple_of(i * rows, rows)
        y = y_ref[pl.ds(r0, rows), :]
        yc = y - jnp.mean(y, axis=-1, keepdims=True)
        yn = yc * lax.rsqrt(jnp.mean(yc * yc, axis=-1, keepdims=True) + EPS)
        yn = yn * lg_ref[...] + lb_ref[...]
        o_ref[pl.ds(r0, rows), :] = _silu(yn).astype(o_ref.dtype)
        return carry

    lax.fori_loop(0, tr // rows, row_block, 0)


def conformer_conv(u, halo, w_dw, ln_g, ln_b, *, n_seq, out_dtype):
    m, d = u.shape
    taps = w_dw.shape[0]
    assert taps - 1 <= CONF_HALO and d % CONF_COLS == 0
    t = m // n_seq
    tr = min(t, 256)
    assert t % tr == 0 and tr % SUBLANES == 0
    tps = t // tr
    prev = halo is None
    if prev:
        assert tr % CONF_HALO == 0
        per = tr // CONF_HALO
        halo_arr = u
        halo_spec = pl.BlockSpec((CONF_HALO, d), lambda i: (jnp.maximum(i * per - 1, 0), 0))
    else:
        assert tps == 1
        halo_arr = halo
        halo_spec = pl.BlockSpec((CONF_HALO, d), lambda i: (i, 0))
    vmem = (2 * _nbytes((tr, d), F32) + 2 * _nbytes((CONF_HALO, d), F32) + 2 * _nbytes((tr, d), out_dtype)
            + _nbytes((2 * tr + CONF_HALO, d), F32) + 4 * _nbytes((32, d), F32))
    return pl.pallas_call(
        functools.partial(_conf_kernel, taps=taps, tiles_per_seq=tps, halo_is_prev_rows=prev),
        grid=(m // tr,),
        in_specs=[pl.BlockSpec((tr, d), lambda i: (i, 0)),
                  halo_spec,
                  pl.BlockSpec((taps, d), lambda i: (0, 0)),
                  pl.BlockSpec((1, d), lambda i: (0, 0)),
                  pl.BlockSpec((1, d), lambda i: (0, 0))],
        out_specs=pl.BlockSpec((tr, d), lambda i: (i, 0)),
        out_shape=jax.ShapeDtypeStruct((m, d), out_dtype),
        scratch_shapes=[pltpu.VMEM((CONF_HALO + tr, d), F32), pltpu.VMEM((tr, d), F32),
                        pltpu.VMEM((SUBLANES, tr + CONF_HALO, CONF_COLS), F32)],
        compiler_params=_params(("arbitrary",), vmem),
        name="conformer_conv",
    )(u, halo_arr, w_dw, ln_g.reshape(1, d), ln_b.reshape(1, d))


def _lane_cumsum(x, seg=None):
    n = x.shape[-1]
    lane = lax.broadcasted_iota(jnp.int32, x.shape, x.ndim - 1)
    pos = lane if seg is None else lane % seg
    d = 1
    while d < (n if seg is None else seg):
        x = x + jnp.where(pos >= d, pltpu.roll(x, d, x.ndim - 1), 0.0)
        d *= 2
    return x


def _logf_kernel(gt_ref, bf_ref, lf_ref, c_ref):
    lf = jax.nn.log_sigmoid(gt_ref[0] + bf_ref[...])
    lf_ref[0] = lf
    c_ref[0] = _lane_cumsum(lf)


def fox_logf(gates_t, b_f):
    n_seq, h, t = gates_t.shape
    spec = pl.BlockSpec((1, h, t), lambda s: (s, 0, 0))
    shape = jax.ShapeDtypeStruct((n_seq, h, t), F32)
    return pl.pallas_call(
        _logf_kernel,
        grid=(n_seq,),
        in_specs=[spec, pl.BlockSpec((h, 1), lambda s: (0, 0))],
        out_specs=[spec, spec],
        out_shape=[shape, shape],
        compiler_params=_params(("parallel",), 1 << 20),
        name="fox_logf",
    )(gates_t, b_f.reshape(h, 1))


FLASH_HEADS = 2


def _flash_step(q, k_ref, v_ref, ck_ref, cq, hh, carry, j, *, tk, scale, q0, masked):
    m, l, acc = carry
    ks = pl.multiple_of(j * tk, tk)
    cols = slice(hh * HEAD, (hh + 1) * HEAD)
    k = k_ref[pl.ds(ks, tk), cols].astype(BF16)
    v = v_ref[pl.ds(ks, tk), cols].astype(BF16)
    s = _dot_nt(q, k) * scale
    s = s + (cq - ck_ref[0, hh, :, pl.ds(ks, tk)])
    if masked:
        qpos = q0 + lax.broadcasted_iota(jnp.int32, s.shape, 0)
        kpos = ks + lax.broadcasted_iota(jnp.int32, s.shape, 1)
        s = jnp.where(kpos <= qpos, s, -jnp.inf)
    m_new = jnp.maximum(m, jnp.max(s, axis=-1, keepdims=True))
    alpha = jnp.exp(m - m_new)
    p = jnp.exp(s - m_new)
    l = alpha * l + jnp.sum(p, axis=-1, keepdims=True)
    acc = alpha * acc + _dot(p.astype(BF16), v)
    return m_new, l, acc


def _flash_kernel(q_ref, k_ref, v_ref, cq_ref, ck_ref, o_ref, *, scale):
    tq = q_ref.shape[0]
    nh = q_ref.shape[1] // HEAD
    qi = pl.program_id(2)
    steps = [functools.partial(_flash_step, q_ref[:, hh * HEAD:(hh + 1) * HEAD].astype(BF16), k_ref, v_ref, ck_ref,
                               cq_ref[0, hh], hh, tk=tq, scale=scale, q0=qi * tq) for hh in range(nh)]
    init = (jnp.full((tq, 1), -jnp.inf, F32), jnp.zeros((tq, 1), F32), jnp.zeros((tq, HEAD), F32))
    every = lambda carries, j, masked: tuple(step(c, j, masked=masked) for step, c in zip(steps, carries))
    carries = lax.fori_loop(0, qi, lambda j, cs: every(cs, j, False), (init,) * nh)
    for hh, (m, l, acc) in enumerate(every(carries, qi, True)):
        o_ref[:, hh * HEAD:(hh + 1) * HEAD] = (acc / l).astype(o_ref.dtype)


def fox_prompt_attention(q, k, v, c, *, n_seq):
    m, width = q.shape
    h = width // HEAD
    t = m // n_seq
    tq = min(t, 512)
    nq = t // tq
    cq = c.reshape(n_seq, h, t, 1)
    ck = c.reshape(n_seq, h, 1, t)
    nh = FLASH_HEADS if h % FLASH_HEADS == 0 else 1
    wblk = nh * HEAD
    vmem = (4 * _nbytes((tq, wblk), F32) + 4 * _nbytes((t, wblk), F32) + 2 * nh * _nbytes((tq, LANES), F32)
            + 8 * nh * _nbytes((tq, tq), F32))
    return pl.pallas_call(
        functools.partial(_flash_kernel, scale=HEAD ** -0.5),
        grid=(n_seq, h // nh, nq),
        in_specs=[pl.BlockSpec((tq, wblk), lambda b, hh, i: (b * nq + i, hh)),
                  pl.BlockSpec((t, wblk), lambda b, hh, i: (b, hh)),
                  pl.BlockSpec((t, wblk), lambda b, hh, i: (b, hh)),
                  pl.BlockSpec((1, nh, tq, 1), lambda b, hh, i: (b, hh, i, 0)),
                  pl.BlockSpec((1, nh, 1, t), lambda b, hh, i: (b, hh, 0, 0))],
        out_specs=pl.BlockSpec((tq, wblk), lambda b, hh, i: (b * nq + i, hh)),
        out_shape=jax.ShapeDtypeStruct((m, width), BF16),
        compiler_params=_params(("parallel", "parallel", "arbitrary"), vmem),
        name="fox_prompt_attention",
    )(q, k, v, cq, ck)


def _split3(x):
    hi = x.astype(BF16)
    r = x - hi.astype(F32)
    mid = r.astype(BF16)
    lo = (r - mid.astype(F32)).astype(BF16)
    return hi, mid, lo


def _decode_attend(kget, vget, past_t, causal, qs_ref, cn_ref, s_ref, rep_ref, pv_ref, m_ref, l_ref, acc_ref, *,
                   heads, t, scale):
    keys = past_t.shape[1]
    for hh in range(heads):
        rows = slice(hh * t, (hh + 1) * t)
        s_ref[rows, :] = _dot_nt(qs_ref[rows, :].astype(BF16), kget(hh))
        rep_ref[rows, :] = jnp.broadcast_to(past_t[hh:hh + 1, :], (t, keys))
    s = s_ref[...] * scale + (cn_ref[0] - rep_ref[...])
    if causal:
        qpos = lax.broadcasted_iota(jnp.int32, s.shape, 0) % t
        kpos = lax.broadcasted_iota(jnp.int32, s.shape, 1)
        s = jnp.where(kpos <= qpos, s, -jnp.inf)
    m = m_ref[...]
    m_new = jnp.maximum(m, jnp.max(s, axis=-1, keepdims=True))
    alpha = jnp.exp(m - m_new)
    p = jnp.exp(s - m_new)
    l_ref[...] = alpha * l_ref[...] + jnp.sum(p, axis=-1, keepdims=True)
    m_ref[...] = m_new
    s_ref[...] = p
    for hh in range(heads):
        rows = slice(hh * t, (hh + 1) * t)
        pv_ref[rows, :] = _dot(s_ref[rows, :].astype(BF16), vget(hh))
    acc_ref[...] = alpha * acc_ref[...] + pv_ref[...]


def _decode_kernel(pt_ref, q_ref, kp_ref, vp_ref, lf_ref, kn_ref, vn_ref, cn_ref, cnrow_ref, o_ref,
                   qs_ref, s_ref, rep_ref, pv_ref, m_ref, l_ref, acc_ref, carry_ref, *, heads, t, scale):
    del pt_ref
    p = pl.program_id(1)

    @pl.when(p == 0)
    def _():
        m_ref[...] = jnp.full(m_ref.shape, -jnp.inf, F32)
        l_ref[...] = jnp.zeros(l_ref.shape, F32)
        acc_ref[...] = jnp.zeros(acc_ref.shape, F32)
        carry_ref[...] = jnp.zeros(carry_ref.shape, F32)
        for hh in range(heads):
            qs_ref[hh * t:(hh + 1) * t, :] = q_ref[:, hh * HEAD:(hh + 1) * HEAD]

    attend = functools.partial(_decode_attend, qs_ref=qs_ref, cn_ref=cn_ref, s_ref=s_ref, rep_ref=rep_ref,
                               pv_ref=pv_ref, m_ref=m_ref, l_ref=l_ref, acc_ref=acc_ref,
                               heads=heads, t=t, scale=scale)

    x = lf_ref[...]
    n = x.shape[0]
    row = lax.broadcasted_iota(jnp.int32, (n, n), 0)
    col = lax.broadcasted_iota(jnp.int32, (n, n), 1)
    after = jnp.where(row > col, 1.0, 0.0).astype(BF16)
    ones = jnp.ones((n, n), BF16)
    hi, mid, lo = _split3(x)
    later_t = (_dot_tn(hi, after) + _dot_tn(mid, after)) + _dot_tn(lo, after)
    past_t = -(later_t + carry_ref[...])
    carry_ref[...] = carry_ref[...] + ((_dot_tn(hi, ones) + _dot_tn(mid, ones)) + _dot_tn(lo, ones))
    attend(lambda hh: kp_ref[pl.ds(hh, n, stride=heads), :].astype(BF16),
           lambda hh: vp_ref[pl.ds(hh, n, stride=heads), :].astype(BF16), past_t, False)

    @pl.when(p == pl.num_programs(1) - 1)
    def _():
        attend(lambda hh: kn_ref[:, hh * HEAD:(hh + 1) * HEAD].astype(BF16),
               lambda hh: vn_ref[:, hh * HEAD:(hh + 1) * HEAD].astype(BF16), cnrow_ref[...], True)
        for hh in range(heads):
            rows = slice(hh * t, (hh + 1) * t)
            o_ref[:, hh * HEAD:(hh + 1) * HEAD] = acc_ref[rows, :] / l_ref[rows, :]


def fox_sample_attention(q, k_new, v_new, c_new, cache_k, cache_v, cache_logf, slot, page_table):
    n_seq, n_pages = page_table.shape
    m, width = q.shape
    h = width // HEAD
    t = m // n_seq
    slots, pool, page = cache_k.shape[:3]
    cache_k = cache_k.reshape(slots, pool, page * h, HEAD)
    cache_v = cache_v.reshape(slots, pool, page * h, HEAD)
    pad = lambda a: jnp.pad(a.reshape(n_seq, t, width), ((0, 0), (0, page - t), (0, 0)))
    cn_col = c_new[:, :, :t].reshape(n_seq, h * t, 1)
    cn_row = jnp.pad(c_new[:, :, :t], ((0, 0), (0, 0), (0, page - t)))
    last = n_pages - 1
    grid_spec = pltpu.PrefetchScalarGridSpec(
        num_scalar_prefetch=1,
        grid=(n_seq, n_pages),
        in_specs=[pl.BlockSpec((t, width), lambda b, p, pt: (b, 0)),
                  pl.BlockSpec((None, None, page * h, HEAD), lambda b, p, pt: (slot, pt[b, last - p], 0, 0)),
                  pl.BlockSpec((None, None, page * h, HEAD), lambda b, p, pt: (slot, pt[b, last - p], 0, 0)),
                  pl.BlockSpec((None, None, page, h), lambda b, p, pt: (slot, pt[b, last - p], 0, 0)),
                  pl.BlockSpec((None, page, width), lambda b, p, pt: (b, 0, 0)),
                  pl.BlockSpec((None, page, width), lambda b, p, pt: (b, 0, 0)),
                  pl.BlockSpec((1, h * t, 1), lambda b, p, pt: (b, 0, 0)),
                  pl.BlockSpec((None, h, page), lambda b, p, pt: (b, 0, 0))],
        out_specs=pl.BlockSpec((t, width), lambda b, p, pt: (b, 0)),
        scratch_shapes=[pltpu.VMEM((h * t, HEAD), F32), pltpu.VMEM((h * t, page), F32),
                        pltpu.VMEM((h * t, page), F32), pltpu.VMEM((h * t, HEAD), F32),
                        pltpu.VMEM((h * t, 1), F32), pltpu.VMEM((h * t, 1), F32),
                        pltpu.VMEM((h * t, HEAD), F32), pltpu.VMEM((h, page), F32)],
    )
    vmem = 8 * _nbytes((page, width), F32) + 2 * _nbytes((page, width), BF16)
    return pl.pallas_call(
        functools.partial(_decode_kernel, heads=h, t=t, scale=HEAD ** -0.5),
        grid_spec=grid_spec,
        out_shape=jax.ShapeDtypeStruct((m, width), F32),
        compiler_params=_params(("arbitrary", "arbitrary"), vmem),
        name="fox_sample_attention",
    )(page_table, q, cache_k, cache_v, cache_logf, pad(k_new), pad(v_new), cn_col, cn_row)


def _gdn_gate_kernel(gt_ref, alog_ref, dtb_ref, beta_ref, gc_ref, *, heads, t_valid):
    g_all = gt_ref[0]
    lane = lax.broadcasted_iota(jnp.int32, (heads, g_all.shape[1]), 1)
    valid = lane < t_valid
    beta = jnp.where(valid, jax.nn.sigmoid(g_all[:heads]), 0.0)
    g = -jnp.exp(alog_ref[...]) * jax.nn.softplus(g_all[heads:] + dtb_ref[...])
    g = jnp.where(valid, g, 0.0)
    beta_ref[0] = beta
    gc_ref[0] = _lane_cumsum(g, GDN_CHUNK)


def gdn_gates(gates_t, a_log, dt_bias, *, t_valid):
    n_seq, h2, t = gates_t.shape
    h = h2 // 2
    out_spec = pl.BlockSpec((1, h, t), lambda s: (s, 0, 0))
    shape = jax.ShapeDtypeStruct((n_seq, h, t), F32)
    return pl.pallas_call(
        functools.partial(_gdn_gate_kernel, heads=h, t_valid=t_valid),
        grid=(n_seq,),
        in_specs=[pl.BlockSpec((1, h2, t), lambda s: (s, 0, 0)),
                  pl.BlockSpec((h, 1), lambda s: (0, 0)),
                  pl.BlockSpec((h, 1), lambda s: (0, 0))],
        out_specs=[out_spec, out_spec],
        out_shape=[shape, shape],
        compiler_params=_params(("parallel",), 1 << 20),
        name="gdn_gates",
    )(gates_t, a_log.reshape(h, 1), dt_bias.reshape(h, 1))


def _row_to_col(row, eye):
    return jnp.sum(jnp.where(eye, row, 0.0), axis=1, keepdims=True)


def _gdn_kernel(q_ref, k_ref, v_ref, z_ref, beta_ref, gc_ref, s0_ref, ng_ref, o_ref, s_ref,
                u_s, wq_s, kd_s, in_s, *, hb, n_chunks):
    c_len = GDN_CHUNK
    n = hb * c_len
    row = lax.broadcasted_iota(jnp.int32, (n, n), 0)
    col = lax.broadcasted_iota(jnp.int32, (n, n), 1)
    eye = row == col
    same_head = (row // c_len) == (col // c_len)
    lower = jnp.logical_and(same_head, col <= row)
    strict = jnp.logical_and(same_head, col < row)
    ng = ng_ref[...]
    heads = range(hb)
    stack = lambda f: jnp.concatenate([f(hh) for hh in heads], axis=0)
    blk = lambda a, hh: a[hh * c_len:(hh + 1) * c_len, hh * HEAD:(hh + 1) * HEAD]

    def last_of(gc_row, hh):
        return gc_row[:, (hh + 1) * c_len - 1:(hh + 1) * c_len]

    def prepare(c, carry):
        r0 = pl.multiple_of(c * c_len, c_len)

        def unit(ref, hh, scale):
            a = ref[pl.ds(r0, c_len), hh * HEAD:(hh + 1) * HEAD]
            return a * lax.rsqrt(jnp.sum(a * a, axis=-1, keepdims=True) + EPS) * scale

        qn = stack(lambda hh: unit(q_ref, hh, HEAD ** -0.5))
        kn = stack(lambda hh: unit(k_ref, hh, 1.0))
        v = stack(lambda hh: v_ref[pl.ds(r0, c_len), hh * HEAD:(hh + 1) * HEAD])
        beta_row = beta_ref[0, 0, pl.ds(c, 1), :]
        gc_row = gc_ref[0, 0, pl.ds(c, 1), :]
        beta_col = _row_to_col(beta_row, eye)
        gc_col = _row_to_col(gc_row, eye)
        gc_last_col = stack(lambda hh: jnp.broadcast_to(last_of(gc_row, hh), (c_len, 1)))
        decay = jnp.exp(jnp.where(lower, gc_col - gc_row, -jnp.inf))
        kn16 = kn.astype(BF16)
        lmat = jnp.where(strict, _dot_nt(kn16, kn16) * decay * beta_col, 0.0)
        inv = jnp.where(eye, 1.0, 0.0) - lmat
        power = lmat
        span = 2
        while span < c_len:
            p16 = power.astype(BF16)
            power = _dot(p16, p16)
            inv = inv + _dot(inv.astype(BF16), power.astype(BF16))
            span *= 2
        e_col = jnp.exp(gc_col)
        rhs = jnp.concatenate([v * beta_col, kn * (beta_col * e_col)], axis=1)
        sol = _dot(inv.astype(BF16), rhs.astype(BF16))
        u_s[c] = sol[:, :HEAD]
        wq_s[c, 0:n, :] = sol[:, HEAD:].astype(BF16)
        wq_s[c, n:2 * n, :] = (qn * e_col).astype(BF16)
        in_s[c] = (_dot_nt(qn.astype(BF16), kn16) * decay).astype(BF16)
        kd_s[c] = (kn * jnp.exp(gc_last_col - gc_col)).astype(BF16)
        return carry

    lax.fori_loop(0, n_chunks, prepare, 0, unroll=2 if n_chunks % 2 == 0 else 1)

    head_of_row = lax.broadcasted_iota(jnp.int32, (n, HEAD), 0) // c_len

    def advance(c, s_wide):
        r0 = pl.multiple_of(c * c_len, c_len)
        r = _dot(wq_s[c], s_wide.astype(BF16))
        w_s = stack(lambda hh: blk(r, hh))
        q_s = stack(lambda hh: blk(r[n:], hh))
        v_new = u_s[c] - w_s
        o = q_s + _dot(in_s[c], v_new.astype(BF16))
        v_wide = jnp.concatenate([jnp.where(head_of_row == hh, v_new, 0.0) for hh in heads], axis=1).astype(BF16)
        gc_row = gc_ref[0, 0, pl.ds(c, 1), :]
        g_last = jnp.concatenate([jnp.broadcast_to(jnp.exp(last_of(gc_row, hh)), (1, HEAD)) for hh in heads], axis=1)
        s_next = s_wide * g_last + _dot_tn(kd_s[c], v_wide)
        for hh in heads:
            cols = slice(hh * HEAD, (hh + 1) * HEAD)
            oh = o[hh * c_len:(hh + 1) * c_len]
            on = oh * lax.rsqrt(jnp.mean(oh * oh, axis=-1, keepdims=True) + EPS) * ng
            z = z_ref[pl.ds(r0, c_len), cols]
            o_ref[pl.ds(r0, c_len), cols] = (on * _silu(z)).astype(o_ref.dtype)
        return s_next

    s_wide = lax.fori_loop(0, n_chunks, advance, jnp.concatenate([s0_ref[0, hh] for hh in heads], axis=1))
    for hh in heads:
        s_ref[0, hh] = s_wide[:, hh * HEAD:(hh + 1) * HEAD]


def gdn_mix(qkv, z, z_col_off, beta, gc, s0, norm_g, *, n_seq, out_dtype):
    m = qkv.shape[0]
    h = qkv.shape[1] // (3 * HEAD)
    t = m // n_seq
    n_chunks = t // GDN_CHUNK
    hb = 4
    assert h % hb == 0
    nhb = h // hb
    wblk = hb * HEAD
    zoff = z_col_off // wblk
    n = hb * GDN_CHUNK
    by_block = lambda a: jnp.transpose(a.reshape(n_seq, nhb, hb, n_chunks, GDN_CHUNK), (0, 1, 3, 2, 4)).reshape(
        n_seq, nhb, n_chunks, n)
    beta, gc = by_block(beta), by_block(gc)
    seq_spec = lambda off: pl.BlockSpec((t, wblk), lambda b, j: (b, j + off))
    gate_spec = pl.BlockSpec((1, 1, n_chunks, n), lambda b, j: (b, j, 0, 0))
    state_spec = pl.BlockSpec((1, hb, HEAD, HEAD), lambda b, j: (b, j, 0, 0))
    scratch = [pltpu.VMEM((n_chunks, n, HEAD), F32), pltpu.VMEM((n_chunks, 2 * n, HEAD), BF16),
               pltpu.VMEM((n_chunks, n, HEAD), BF16), pltpu.VMEM((n_chunks, n, n), BF16)]
    vmem = (2 * 4 * _nbytes((t, wblk), F32) + 2 * _nbytes((t, wblk), out_dtype)
            + _nbytes((n_chunks, n, HEAD), F32) + _nbytes((n_chunks, 3 * n + 2 * n, HEAD), BF16))
    return pl.pallas_call(
        functools.partial(_gdn_kernel, hb=hb, n_chunks=n_chunks),
        grid=(n_seq, nhb),
        in_specs=[seq_spec(0), seq_spec(nhb), seq_spec(2 * nhb), seq_spec(zoff),
                  gate_spec, gate_spec, state_spec,
                  pl.BlockSpec((1, HEAD), lambda b, j: (0, 0))],
        out_specs=[pl.BlockSpec((t, wblk), lambda b, j: (b, j)), state_spec],
        out_shape=[jax.ShapeDtypeStruct((m, h * HEAD), out_dtype),
                   jax.ShapeDtypeStruct((n_seq, h, HEAD, HEAD), F32)],
        scratch_shapes=scratch,
        compiler_params=_params(("parallel", "parallel"), vmem),
        name="gdn_mix",
    )(qkv, qkv, qkv, z, beta, gc, s0, norm_g.reshape(1, HEAD))


def _xattn_q_kernel(x_ref, w_ref, mk_ref, mv_ref, o_ref, wb_ref, *, heads, scale):
    @pl.when(pl.program_id(0) == 0)
    def _():
        wb_ref[...] = w_ref[...].astype(BF16)

    q = _dot(x_ref[...].astype(BF16), wb_ref[...])
    for hh in range(heads):
        cols = slice(hh * HEAD, (hh + 1) * HEAD)
        s = _dot_nt(q[:, cols].astype(BF16), mk_ref[0, :, cols].astype(BF16)) * scale
        p = jnp.exp(s - jnp.max(s, axis=-1, keepdims=True))
        o = _dot(p.astype(BF16), mv_ref[0, :, cols].astype(BF16)) / jnp.sum(p, axis=-1, keepdims=True)
        o_ref[:, cols] = o.astype(o_ref.dtype)


def xattn_heads(x, w_q, mem_k, mem_v, layer, *, n_seq):
    m, d = x.shape
    width = w_q.shape[2]
    t = m // n_seq
    tm = min(t, _row_tile(m, x.dtype, d))
    tps = t // tm
    mem = mem_k.shape[2]
    vmem = (2 * _nbytes((tm, d), x.dtype) + 3 * _nbytes((d, width), F32) + 4 * _nbytes((mem, width), F32)
            + 2 * _nbytes((tm, width), F32) + 8 * _nbytes((tm, mem), F32))
    return pl.pallas_call(
        functools.partial(_xattn_q_kernel, heads=width // HEAD, scale=HEAD ** -0.5),
        grid=(m // tm,),
        in_specs=[pl.BlockSpec((tm, d), lambda i: (i, 0)),
                  pl.BlockSpec((None, d, width), lambda i: (layer, 0, 0)),
                  pl.BlockSpec((None, 1, mem, width), lambda i: (layer, i // tps, 0, 0)),
                  pl.BlockSpec((None, 1, mem, width), lambda i: (layer, i // tps, 0, 0))],
        out_specs=pl.BlockSpec((tm, width), lambda i: (i, 0)),
        out_shape=jax.ShapeDtypeStruct((m, width), F32),
        scratch_shapes=[pltpu.VMEM((d, width), BF16)],
        compiler_params=_params(("arbitrary",), vmem),
        name="xattn_heads",
    )(x, w_q, mem_k, mem_v)


def _xattn_o_kernel(a_ref, w_ref, res_ref, g_ref, x_ref, h_ref, wb_ref):
    @pl.when(pl.program_id(0) == 0)
    def _():
        wb_ref[...] = w_ref[...].astype(BF16)

    x = res_ref[...] + _dot(a_ref[...].astype(BF16), wb_ref[...])
    x_ref[...] = x
    h_ref[...] = _rmsnorm_rows(x, g_ref[...]).astype(h_ref.dtype)


def xattn_out(a, w_o, layer, res, g, h_dtype):
    m, width = a.shape
    d = w_o.shape[2]
    tm = min(m, 256)
    vmem = (2 * _nbytes((tm, width), F32) + 3 * _nbytes((width, d), F32) + 4 * _nbytes((tm, d), F32)
            + 2 * _nbytes((tm, d), h_dtype) + 2 * _nbytes((tm, d), F32))
    return pl.pallas_call(
        _xattn_o_kernel,
        grid=(m // tm,),
        in_specs=[pl.BlockSpec((tm, width), lambda i: (i, 0)),
                  pl.BlockSpec((None, width, d), lambda i: (layer, 0, 0)),
                  pl.BlockSpec((tm, d), lambda i: (i, 0)),
                  pl.BlockSpec((1, d), lambda i: (0, 0))],
        out_specs=[pl.BlockSpec((tm, d), lambda i: (i, 0)), pl.BlockSpec((tm, d), lambda i: (i, 0))],
        out_shape=[jax.ShapeDtypeStruct((m, d), F32), jax.ShapeDtypeStruct((m, d), h_dtype)],
        scratch_shapes=[pltpu.VMEM((width, d), BF16)],
        compiler_params=_params(("arbitrary",), vmem),
        name="xattn_out",
    )(a, w_o, res, g.reshape(1, d))


def _pad_lanes(a, mult):
    pad = (-a.shape[-1]) % mult
    return a if pad == 0 else jnp.pad(a, [(0, 0)] * (a.ndim - 1) + [(0, pad)])


def _trunk(xp, bp, xs, bs, mem_p, mem_s, states, W):
    d = xp.shape[1]
    tp, ts = xp.shape[0] // bp, xs.shape[0] // bs
    depth = W['norm_mix'].shape[0]
    ffn_dim = W['ffn_w_dw'].shape[2]
    heads = d // HEAD
    hv = heads * HEAD
    outs_p = dict(conf=[], k=[], v=[], lf=[], s=[], gbuf=[], ffn=[])
    outs_s = dict(conf=[], k=[], v=[], lf=[], s=[], gbuf=[], ffn=[])

    def logf(gate, n_seq, t, slot):
        gates_t = _pad_lanes(jnp.transpose(gate.reshape(n_seq, t, heads), (0, 2, 1)), LANES)
        lf_t, c_t = fox_logf(gates_t, W['fox_b_f'][slot])
        return jnp.transpose(lf_t[:, :, :t], (0, 2, 1)), c_t

    def gdn_group(qkv, z, gate, s0, n_seq, t, t_pad, slot, out_dtype):
        gates_t = jnp.transpose(gate.reshape(n_seq, t, 2 * heads), (0, 2, 1))
        gates_t = jnp.pad(gates_t, ((0, 0), (0, 0), (0, max(t_pad, LANES) - t)))
        beta, gc = gdn_gates(gates_t, W['gdn_a_log'][slot], W['gdn_dt_bias'][slot], t_valid=t)
        return gdn_mix(qkv, z, 0, beta[:, :, :t_pad], gc[:, :, :t_pad], s0, W['gdn_norm_g'][slot],
                       n_seq=n_seq, out_dtype=out_dtype)

    for i in range(depth):
        kind, slot = i % 3, i // 3
        hp = rmsnorm(xp, W['norm_mix'][i], BF16)
        hs = rmsnorm(xs, W['norm_mix'][i], F32)
        if kind == 0:
            up, us = matmul_glu(hp, W['conf_w_pw1'], slot, hs)
            taps = W['conf_w_dw'].shape[1]
            outs_p['conf'].append(up.reshape(bp, tp, d)[:, tp - (taps - 1):])
            buf = states['conf'][slot]
            halo = jnp.pad(buf, ((0, 0), (CONF_HALO - (taps - 1), 0), (0, 0))).reshape(bs * CONF_HALO, d)
            outs_s['conf'].append(_next_state(buf, us, bs, d))
            conv = functools.partial(conformer_conv, w_dw=W['conf_w_dw'][slot], ln_g=W['conf_ln_g'][slot],
                                     ln_b=W['conf_ln_b'][slot])
            yp = conv(up, None, n_seq=bp, out_dtype=BF16)
            ys = conv(us, halo, n_seq=bs, out_dtype=F32)
            xp, xs = matmul(yp, W['conf_w_pw2'], slot, res=xp, x2=ys, res2=xs)
        elif kind == 1:
            w_in = W['fox_w_in']
            qp, qs = matmul(hp, w_in, slot, col_off=0, n_out=hv, x2=hs)
            kp, ks = matmul(hp, w_in, slot, col_off=hv, n_out=hv, x2=hs)
            vp, vs = matmul(hp, w_in, slot, col_off=2 * hv, n_out=hv, x2=hs)
            gate_p, gate_s = matmul(hp, w_in[slot, :, 3 * hv:], x2=hs)
            lf_p, c_p = logf(gate_p, bp, tp, slot)
            lf_s, c_s = logf(gate_s, bs, ts, slot)
            op = fox_prompt_attention(qp, kp, vp, c_p, n_seq=bp)
            cache_k, cache_v, cache_lf, page_table = states['fox']
            os_ = fox_sample_attention(qs, ks, vs, c_s, cache_k, cache_v, cache_lf, slot, page_table)
            xp, xs = matmul(op, W['fox_w_o'], slot, res=xp, x2=os_, res2=xs)
            for outs, k, v, lf, n_seq, t in ((outs_p, kp, vp, lf_p, bp, tp), (outs_s, ks, vs, lf_s, bs, ts)):
                outs['k'].append(k.reshape(n_seq, t, heads, HEAD))
                outs['v'].append(v.reshape(n_seq, t, heads, HEAD))
                outs['lf'].append(lf)
        else:
            w_in = W['gdn_w_in']
            cw = W['gdn_w_conv']
            n_qkv = cw.shape[2]
            qkv_p, tails, pre_s = matmul_conv(hp, w_in, cw, slot, hs, n_seq=bp, n_out=n_qkv, gated=False,
                                              out_dtype=F32)
            outs_p['gbuf'].append(tails[:, SUBLANES - (cw.shape[1] - 1):])
            zp, zs = matmul(hp, w_in, slot, col_off=n_qkv, n_out=hv, x2=hs)
            gate_p, gate_s = matmul(hp, w_in[slot, :, n_qkv + hv:], x2=hs)
            op, s_p = gdn_group(qkv_p, zp, gate_p, jnp.zeros((bp, heads, HEAD, HEAD), F32), bp, tp, tp, slot, BF16)
            buf = states['gdn_conv'][slot]
            qkv_s = state_conv(pre_s, None, buf, cw[slot])
            outs_s['gbuf'].append(_next_state(buf, pre_s, bs, n_qkv))
            ts_pad = -(-ts // GDN_CHUNK) * GDN_CHUNK
            rows = lambda a: jnp.pad(a.reshape(bs, ts, -1), ((0, 0), (0, ts_pad - ts), (0, 0))).reshape(
                bs * ts_pad, -1)
            os_, s_s = gdn_group(rows(qkv_s), rows(zs), gate_s, states['gdn'][slot], bs, ts, ts_pad, slot, F32)
            os_ = os_.reshape(bs, ts_pad, hv)[:, :ts].reshape(bs * ts, hv)
            xp, xs = matmul(op, W['gdn_w_o'], slot, res=xp, x2=os_, res2=xs)
            outs_p['s'].append(s_p)
            outs_s['s'].append(s_s)
        ap = xattn_heads(rmsnorm(xp, W['norm_mem'][i], BF16), W['x_w_q'], mem_p[0], mem_p[1], i, n_seq=bp)
        as_ = xattn_heads(rmsnorm(xs, W['norm_mem'][i], F32), W['x_w_q'], mem_s[0], mem_s[1], i, n_seq=bs)
        xp, hp = xattn_out(ap, W['x_w_o'], i, xp, W['norm_ffn'][i], BF16)
        xs, hs = xattn_out(as_, W['x_w_o'], i, xs, W['norm_ffn'][i], F32)
        cw = W['ffn_w_dw']
        yp, tails, pre_g, pre_v = matmul_conv(hp, W['ffn_w_up'], cw, i, hs, n_seq=bp, n_out=ffn_dim, gated=True,
                                              out_dtype=BF16)
        outs_p['ffn'].append(tails[:, SUBLANES - (cw.shape[1] - 1):])
        buf = states['ffn_conv'][i]
        ys = state_conv(pre_g, pre_v, buf, cw[i])
        outs_s['ffn'].append(_next_state(buf, pre_g, bs, ffn_dim))
        xp, xs = matmul(yp, W['ffn_w_down'], i, res=xp, x2=ys, res2=xs)
    finish = lambda x, n_seq, t, outs: (
        rmsnorm(x, W['norm_final'], F32).reshape(n_seq, t, d),
        *(jnp.stack(outs[name]) for name in ('conf', 'k', 'v', 'lf', 's', 'gbuf', 'ffn')))
    return finish(xp, bp, tp, outs_p), finish(xs, bs, ts, outs_s)


def kernel(x_prompt, x_sample, cache_fox_k, cache_fox_v, cache_fox_logf, cache_mem_k, cache_mem_v, state_conf,
           state_gdn, state_gdn_conv, state_ffn_conv, page_table, mem_prompt, norm_mix, norm_mem, norm_ffn,
           norm_final, conf_w_pw1, conf_w_dw, conf_ln_g, conf_ln_b, conf_w_pw2, fox_w_in, fox_b_f, fox_w_o,
           gdn_w_in, gdn_w_conv, gdn_a_log, gdn_dt_bias, gdn_norm_g, gdn_w_o, x_w_q, x_w_kv, x_w_o, ffn_w_up,
           ffn_w_dw, ffn_w_down):
    W = dict(norm_mix=norm_mix, norm_mem=norm_mem, norm_ffn=norm_ffn, norm_final=norm_final,
             conf_w_pw1=conf_w_pw1, conf_w_dw=conf_w_dw, conf_ln_g=conf_ln_g, conf_ln_b=conf_ln_b,
             conf_w_pw2=conf_w_pw2, fox_w_in=fox_w_in, fox_b_f=fox_b_f, fox_w_o=fox_w_o,
             gdn_w_in=gdn_w_in, gdn_w_conv=gdn_w_conv, gdn_a_log=gdn_a_log, gdn_dt_bias=gdn_dt_bias,
             gdn_norm_g=gdn_norm_g, gdn_w_o=gdn_w_o, x_w_q=x_w_q, x_w_o=x_w_o,
             ffn_w_up=ffn_w_up, ffn_w_dw=ffn_w_dw, ffn_w_down=ffn_w_down)
    b, t, d = x_prompt.shape
    bs, ts, _ = x_sample.shape
    depth = x_w_kv.shape[0]
    mem = mem_prompt.shape[1]
    xw = x_w_kv.shape[2] // 2
    xh = xw // HEAD

    mem_rows = mem_prompt.reshape(b * mem, d)
    kv = jnp.stack([matmul(mem_rows, x_w_kv, i) for i in range(depth)])
    mem_k_p = kv[..., :xw].reshape(depth, b, mem, xw)
    mem_v_p = kv[..., xw:].reshape(depth, b, mem, xw)

    states = dict(conf=state_conf, gdn=state_gdn, gdn_conv=state_gdn_conv, ffn_conv=state_ffn_conv,
                  fox=(cache_fox_k, cache_fox_v, cache_fox_logf, page_table))
    mem_s = (cache_mem_k.reshape(depth, bs, mem, xw), cache_mem_v.reshape(depth, bs, mem, xw))
    ((y_p, conf_p, fox_k_p, fox_v_p, fox_lf_p, gdn_p, gdn_conv_p, ffn_conv_p),
     (y_s, conf_s, fox_k_s, fox_v_s, fox_lf_s, gdn_s, gdn_conv_s, ffn_conv_s)) = _trunk(
        x_prompt.reshape(b * t, d), b, x_sample.reshape(bs * ts, d), bs, (mem_k_p, mem_v_p), mem_s, states, W)

    return (y_p, y_s,
            conf_p, fox_k_p, fox_v_p, fox_lf_p, gdn_p, gdn_conv_p,
            mem_k_p.reshape(depth, b, mem, xh, HEAD), mem_v_p.reshape(depth, b, mem, xh, HEAD), ffn_conv_p,
            conf_s, fox_k_s, fox_v_s, fox_lf_s, gdn_s, gdn_conv_s, ffn_conv_s)
```

```python
import functools

import jax
import jax.numpy as jnp
from jax import lax
from jax.experimental import pallas as pl
from jax.experimental.pallas import tpu as pltpu

EPS = 1e-6
HEAD = 128
GDN_CHUNK = 64
SUBLANES = 8
LANES = 128
VMEM_CAP = 58 * 1024 * 1024
BF16 = jnp.bfloat16
F32 = jnp.float32


def _params(semantics, vmem_bytes):
    limit = int(min(VMEM_CAP, max(vmem_bytes * 5 // 4 + (4 << 20), 16 << 20)))
    return pltpu.CompilerParams(dimension_semantics=semantics, vmem_limit_bytes=limit)


def _nbytes(shape, dtype):
    n = 1
    for s in shape:
        n *= s
    return n * jnp.dtype(dtype).itemsize


def _silu(x):
    return x * jax.nn.sigmoid(x)


def _dot(a, b):
    return jnp.dot(a, b, preferred_element_type=F32)


def _dot_nt(a, b):
    return lax.dot_general(a, b, (((1,), (1,)), ((), ())), preferred_element_type=F32)


def _dot_tn(a, b):
    return lax.dot_general(a, b, (((0,), (0,)), ((), ())), preferred_element_type=F32)


def _rmsnorm_rows(x, g):
    return x * lax.rsqrt(jnp.mean(x * x, axis=-1, keepdims=True) + EPS) * g


def _rmsnorm_kernel(x_ref, g_ref, o_ref):
    o_ref[...] = _rmsnorm_rows(x_ref[...], g_ref[...]).astype(o_ref.dtype)


def rmsnorm(x, g, out_dtype):
    m, d = x.shape
    tm = min(m, 256)
    vmem = 2 * (_nbytes((tm, d), F32) + _nbytes((tm, d), out_dtype))
    return pl.pallas_call(
        _rmsnorm_kernel,
        grid=(m // tm,),
        in_specs=[pl.BlockSpec((tm, d), lambda i: (i, 0)),
                  pl.BlockSpec((1, d), lambda i: (0, 0))],
        out_specs=pl.BlockSpec((tm, d), lambda i: (i, 0)),
        out_shape=jax.ShapeDtypeStruct((m, d), out_dtype),
        compiler_params=_params(("parallel",), vmem),
        name="rmsnorm",
    )(x, g.reshape(1, d))


def _row_tile(m, x_dtype, k):
    cap = 1024 if jnp.dtype(x_dtype).itemsize == 2 else 512
    if k > 4096:
        cap //= 4
    tm = min(m, cap)
    assert m % tm == 0
    return tm


def _col_tile(n, cap):
    tn = min(n, cap)
    assert n % tn == 0
    return tn


def _mm_kernel(*refs, has_res, has_x2):
    refs = list(refs)
    x_ref, w_ref = refs[:2]
    del refs[:2]
    res_ref = refs.pop(0) if has_res else None
    x2_ref = refs.pop(0) if has_x2 else None
    res2_ref = refs.pop(0) if has_x2 and has_res else None
    o_ref = refs.pop(0)
    o2_ref = refs.pop(0) if has_x2 else None
    wb_ref, = refs

    @pl.when(pl.program_id(1) == 0)
    def _():
        wb_ref[...] = w_ref[...].astype(BF16)
        if has_x2:
            acc2 = _dot(x2_ref[...].astype(BF16), wb_ref[...])
            if has_res:
                acc2 = res2_ref[...] + acc2
            o2_ref[...] = acc2

    acc = _dot(x_ref[...].astype(BF16), wb_ref[...])
    if has_res:
        acc = res_ref[...] + acc
    o_ref[...] = acc.astype(o_ref.dtype)


def _stacked(w, layer):
    return (w[None], 0) if w.ndim == 2 else (w, layer)


def _x2_spec(x2):
    return pl.BlockSpec(x2.shape, lambda n, i: (0, 0), pipeline_mode=pl.Buffered(1))


def matmul(x, w, layer=0, *, col_off=0, n_out=None, res=None, out_dtype=F32, x2=None, res2=None):
    w, layer = _stacked(w, layer)
    m, k = x.shape
    n_out = w.shape[2] - col_off if n_out is None else n_out
    deep = k > 4096
    tn = _col_tile(n_out, 512)
    assert col_off % tn == 0
    off = col_off // tn
    tm = _row_tile(m, x.dtype, k)
    w_bufs = 1 if deep else 2
    w_mode = dict(pipeline_mode=pl.Buffered(1)) if deep else {}
    tile = pl.BlockSpec((tm, tn), lambda n, i: (i, n))
    in_specs = [pl.BlockSpec((tm, k), lambda n, i: (i, 0)),
                pl.BlockSpec((None, k, tn), lambda n, i: (layer, 0, n + off), **w_mode)]
    args = [x, w]
    out_specs = [tile]
    out_shape = [jax.ShapeDtypeStruct((m, n_out), out_dtype)]
    vmem = (2 * _nbytes((tm, k), x.dtype) + w_bufs * _nbytes((k, tn), F32) + _nbytes((k, tn), BF16)
            + 2 * _nbytes((tm, tn), out_dtype))
    if res is not None:
        in_specs.append(tile)
        args.append(res)
        vmem += 2 * _nbytes((tm, tn), F32)
    if x2 is not None:
        m2 = x2.shape[0]
        tile2 = pl.BlockSpec((m2, tn), lambda n, i: (0, n))
        in_specs.append(_x2_spec(x2))
        args.append(x2)
        if res is not None:
            in_specs.append(tile2)
            args.append(res2)
        out_specs.append(tile2)
        out_shape.append(jax.ShapeDtypeStruct((m2, n_out), F32))
        vmem += _nbytes(x2.shape, x2.dtype) + 4 * _nbytes((m2, tn), F32)
    out = pl.pallas_call(
        functools.partial(_mm_kernel, has_res=res is not None, has_x2=x2 is not None),
        grid=(n_out // tn, m // tm),
        in_specs=in_specs,
        out_specs=out_specs,
        out_shape=out_shape,
        scratch_shapes=[pltpu.VMEM((k, tn), BF16)],
        compiler_params=_params(("arbitrary", "arbitrary"), vmem),
        name="matmul",
    )(*args)
    return out[0] if x2 is None else out


def _glu_kernel(x_ref, wa_ref, wb_ref, x2_ref, o_ref, o2_ref, wab_ref, wbb_ref):
    glu = lambda x: _dot(x, wab_ref[...]) * jax.nn.sigmoid(_dot(x, wbb_ref[...]))

    @pl.when(pl.program_id(1) == 0)
    def _():
        wab_ref[...] = wa_ref[...].astype(BF16)
        wbb_ref[...] = wb_ref[...].astype(BF16)
        o2_ref[...] = glu(x2_ref[...].astype(BF16))

    o_ref[...] = glu(x_ref[...].astype(BF16))


def matmul_glu(x, w, layer, x2):
    m, k = x.shape
    m2 = x2.shape[0]
    n = w.shape[2] // 2
    tn = _col_tile(n, 256)
    tm = _row_tile(m, x.dtype, k)
    nb = n // tn
    vmem = (2 * _nbytes((tm, k), x.dtype) + 4 * _nbytes((k, tn), F32) + 2 * _nbytes((k, tn), BF16)
            + 2 * _nbytes((tm, tn), F32) + _nbytes(x2.shape, x2.dtype) + 2 * _nbytes((m2, tn), F32))
    return pl.pallas_call(
        _glu_kernel,
        grid=(nb, m // tm),
        in_specs=[pl.BlockSpec((tm, k), lambda j, i: (i, 0)),
                  pl.BlockSpec((None, k, tn), lambda j, i: (layer, 0, j)),
                  pl.BlockSpec((None, k, tn), lambda j, i: (layer, 0, j + nb)),
                  _x2_spec(x2)],
        out_specs=[pl.BlockSpec((tm, tn), lambda j, i: (i, j)), pl.BlockSpec((m2, tn), lambda j, i: (0, j))],
        out_shape=[jax.ShapeDtypeStruct((m, n), F32), jax.ShapeDtypeStruct((m2, n), F32)],
        scratch_shapes=[pltpu.VMEM((k, tn), BF16), pltpu.VMEM((k, tn), BF16)],
        compiler_params=_params(("arbitrary", "arbitrary"), vmem),
        name="matmul_glu",
    )(x, w, w, x2)


def _mm_conv_kernel(*refs, width, tiles_per_seq, gated):
    if gated:
        x_ref, wg_ref, wv_ref, cw_ref, x2_ref, y_ref, tail_ref, g2_ref, v2_ref, wgb_ref, wvb_ref, ext_ref = refs
    else:
        x_ref, wg_ref, cw_ref, x2_ref, y_ref, tail_ref, g2_ref, wgb_ref, ext_ref = refs
    i = pl.program_id(1)

    @pl.when(i == 0)
    def _():
        x2 = x2_ref[...].astype(BF16)
        wgb_ref[...] = wg_ref[...].astype(BF16)
        g2_ref[...] = _dot(x2, wgb_ref[...])
        if gated:
            wvb_ref[...] = wv_ref[...].astype(BF16)
            v2_ref[...] = _dot(x2, wvb_ref[...])

    @pl.when(i % tiles_per_seq == 0)
    def _():
        ext_ref[0:SUBLANES, :] = jnp.zeros((SUBLANES, ext_ref.shape[1]), F32)

    x = x_ref[...].astype(BF16)
    g = _dot(x, wgb_ref[...])
    tm = g.shape[0]
    ext_ref[SUBLANES:SUBLANES + tm, :] = g
    cw = cw_ref[...]
    y = g * cw[width - 1:width, :]
    for j in range(width - 1):
        y = y + ext_ref[pl.ds(SUBLANES - (width - 1) + j, tm), :] * cw[j:j + 1, :]
    y = _silu(y)
    if gated:
        y = y * _dot(x, wvb_ref[...])
    y_ref[...] = y.astype(y_ref.dtype)
    tail = g[tm - SUBLANES:tm, :]
    tail_ref[0] = tail
    ext_ref[0:SUBLANES, :] = tail


def matmul_conv(x, w, cw, layer, x2, *, n_seq, n_out, gated, out_dtype):
    m, k = x.shape
    m2 = x2.shape[0]
    width = cw.shape[1]
    tn = _col_tile(n_out, 256 if gated else 512)
    t = m // n_seq
    tm = _row_tile(t, x.dtype, k)
    tps = t // tm
    nb = n_out // tn
    in_specs = [pl.BlockSpec((tm, k), lambda j, i: (i, 0)),
                pl.BlockSpec((None, k, tn), lambda j, i: (layer, 0, j))]
    args = [x, w]
    scratch = [pltpu.VMEM((k, tn), BF16)]
    if gated:
        in_specs.append(pl.BlockSpec((None, k, tn), lambda j, i: (layer, 0, j + nb)))
        args.append(w)
        scratch.append(pltpu.VMEM((k, tn), BF16))
    in_specs.append(pl.BlockSpec((None, width, tn), lambda j, i: (layer, 0, j)))
    args.append(cw)
    in_specs.append(_x2_spec(x2))
    args.append(x2)
    scratch.append(pltpu.VMEM((tm + SUBLANES, tn), F32))
    nw = 2 if gated else 1
    vmem = (2 * _nbytes((tm, k), x.dtype) + nw * (2 * _nbytes((k, tn), F32) + _nbytes((k, tn), BF16))
            + 2 * _nbytes((tm, tn), out_dtype) + 6 * _nbytes((tm, tn), F32)
            + _nbytes(x2.shape, x2.dtype) + 2 * nw * _nbytes((m2, tn), F32))
    tile2 = pl.BlockSpec((m2, tn), lambda j, i: (0, j))
    plain2 = jax.ShapeDtypeStruct((m2, n_out), F32)
    return pl.pallas_call(
        functools.partial(_mm_conv_kernel, width=width, tiles_per_seq=tps, gated=gated),
        grid=(nb, m // tm),
        in_specs=in_specs,
        out_specs=[pl.BlockSpec((tm, tn), lambda j, i: (i, j)),
                   pl.BlockSpec((1, SUBLANES, tn), lambda j, i: (i // tps, 0, j))] + [tile2] * nw,
        out_shape=[jax.ShapeDtypeStruct((m, n_out), out_dtype),
                   jax.ShapeDtypeStruct((n_seq, SUBLANES, n_out), F32)] + [plain2] * nw,
        scratch_shapes=scratch,
        compiler_params=_params(("arbitrary", "arbitrary"), vmem),
        name="matmul_conv",
    )(*args)


def _state_conv_kernel(*refs, width, gated):
    if gated:
        u_ref, v_ref, st_ref, cw_ref, y_ref, ext_ref = refs
    else:
        u_ref, st_ref, cw_ref, y_ref, ext_ref = refs
    t = u_ref.shape[0]
    ext_ref[SUBLANES - (width - 1):SUBLANES, :] = st_ref[0]
    ext_ref[SUBLANES:SUBLANES + t, :] = u_ref[...]
    cw = cw_ref[...]
    y = jnp.zeros(u_ref.shape, F32)
    for j in range(width):
        y = y + ext_ref[pl.ds(SUBLANES - (width - 1) + j, t), :] * cw[j:j + 1, :]
    y = _silu(y)
    if gated:
        y = y * v_ref[...]
    y_ref[...] = y


def state_conv(u, v, state, cw):
    n_seq, wm1, n_out = state.shape
    t = u.shape[0] // n_seq
    width = wm1 + 1
    gated = v is not None
    tc = _col_tile(n_out, 256)
    nb = n_out // tc
    in_specs = [pl.BlockSpec((t, tc), lambda s, j: (s, j))]
    args = [u]
    if gated:
        in_specs.append(pl.BlockSpec((t, tc), lambda s, j: (s, j)))
        args.append(v)
    in_specs += [pl.BlockSpec((1, wm1, tc), lambda s, j: (s, 0, j)),
                 pl.BlockSpec((width, tc), lambda s, j: (0, j))]
    args += [state, cw]
    return pl.pallas_call(
        functools.partial(_state_conv_kernel, width=width, gated=gated),
        grid=(n_seq, nb),
        in_specs=in_specs,
        out_specs=pl.BlockSpec((t, tc), lambda s, j: (s, j)),
        out_shape=jax.ShapeDtypeStruct((n_seq * t, n_out), F32),
        scratch_shapes=[pltpu.VMEM((SUBLANES + t, tc), F32)],
        compiler_params=_params(("parallel", "parallel"), 1 << 20),
        name="state_conv",
    )(*args)


def _next_state(state, u, n_seq, n_out):
    wm1 = state.shape[1]
    ext = jnp.concatenate([state, u[:, :n_out].reshape(n_seq, -1, n_out)], axis=1)
    return ext[:, ext.shape[1] - wm1:]


CONF_HALO = 32
CONF_ROWS = 64
CONF_COLS = 256


def _conf_kernel(u_ref, halo_ref, w_ref, lg_ref, lb_ref, o_ref, ext_ref, y_ref, sh_ref, *,
                 taps, tiles_per_seq, halo_is_prev_rows):
    tr, d = u_ref.shape
    if halo_is_prev_rows:
        first = pl.program_id(0) % tiles_per_seq == 0

        @pl.when(first)
        def _():
            ext_ref[0:CONF_HALO, :] = jnp.zeros((CONF_HALO, d), F32)

        @pl.when(jnp.logical_not(first))
        def _():
            ext_ref[0:CONF_HALO, :] = halo_ref[...]
    else:
        ext_ref[0:CONF_HALO, :] = halo_ref[...]
    ext_ref[CONF_HALO:CONF_HALO + tr, :] = u_ref[...]

    rows = min(tr, CONF_ROWS)
    lead = CONF_HALO - (taps - 1)

    def col_block(c, carry):
        c0 = pl.multiple_of(c * CONF_COLS, CONF_COLS)
        w = w_ref[:, pl.ds(c0, CONF_COLS)]
        for r in range(min(SUBLANES, taps)):
            span = tr + (len(range(r, taps, SUBLANES)) - 1) * SUBLANES
            sh_ref[r, 0:span, :] = ext_ref[pl.ds(lead + r, span), pl.ds(c0, CONF_COLS)]
        for r0 in range(0, tr, rows):
            acc = jnp.zeros((rows, CONF_COLS), F32)
            for j in range(taps):
                r = j % SUBLANES
                acc = acc + sh_ref[r, r0 + j - r:r0 + j - r + rows, :] * w[j:j + 1, :]
            y_ref[r0:r0 + rows, pl.ds(c0, CONF_COLS)] = acc
        return carry

    lax.fori_loop(0, d // CONF_COLS, col_block, 0)

    def row_block(i, carry):
        r0 = pl.multiple_of(i * rows, rows)
        y = y_ref[pl.ds(r0, rows), :]
        yc = y - jnp.mean(y, axis=-1, keepdims=True)
        yn = yc * lax.rsqrt(jnp.mean(yc * yc, axis=-1, keepdims=True) + EPS)
        yn = yn * lg_ref[...] + lb_ref[...]
        o_ref[pl.ds(r0, rows), :] = _silu(yn).astype(o_ref.dtype)
        return carry

    lax.fori_loop(0, tr // rows, row_block, 0)


def conformer_conv(u, halo, w_dw, ln_g, ln_b, *, n_seq, out_dtype):
    m, d = u.shape
    taps = w_dw.shape[0]
    assert taps - 1 <= CONF_HALO and d % CONF_COLS == 0
    t = m // n_seq
    tr = min(t, 256)
    assert t % tr == 0 and tr % SUBLANES == 0
    tps = t // tr
    prev = halo is None
    if prev:
        assert tr % CONF_HALO == 0
        per = tr // CONF_HALO
        halo_arr = u
        halo_spec = pl.BlockSpec((CONF_HALO, d), lambda i: (jnp.maximum(i * per - 1, 0), 0))
    else:
        assert tps == 1
        halo_arr = halo
        halo_spec = pl.BlockSpec((CONF_HALO, d), lambda i: (i, 0))
    vmem = (2 * _nbytes((tr, d), F32) + 2 * _nbytes((CONF_HALO, d), F32) + 2 * _nbytes((tr, d), out_dtype)
            + _nbytes((2 * tr + CONF_HALO, d), F32) + 4 * _nbytes((32, d), F32))
    return pl.pallas_call(
        functools.partial(_conf_kernel, taps=taps, tiles_per_seq=tps, halo_is_prev_rows=prev),
        grid=(m // tr,),
        in_specs=[pl.BlockSpec((tr, d), lambda i: (i, 0)),
                  halo_spec,
                  pl.BlockSpec((taps, d), lambda i: (0, 0)),
                  pl.BlockSpec((1, d), lambda i: (0, 0)),
                  pl.BlockSpec((1, d), lambda i: (0, 0))],
        out_specs=pl.BlockSpec((tr, d), lambda i: (i, 0)),
        out_shape=jax.ShapeDtypeStruct((m, d), out_dtype),
        scratch_shapes=[pltpu.VMEM((CONF_HALO + tr, d), F32), pltpu.VMEM((tr, d), F32),
                        pltpu.VMEM((SUBLANES, tr + CONF_HALO, CONF_COLS), F32)],
        compiler_params=_params(("arbitrary",), vmem),
        name="conformer_conv",
    )(u, halo_arr, w_dw, ln_g.reshape(1, d), ln_b.reshape(1, d))


def _lane_cumsum(x, seg=None):
    n = x.shape[-1]
    lane = lax.broadcasted_iota(jnp.int32, x.shape, x.ndim - 1)
    pos = lane if seg is None else lane % seg
    d = 1
    while d < (n if seg is None else seg):
        x = x + jnp.where(pos >= d, pltpu.roll(x, d, x.ndim - 1), 0.0)
        d *= 2
    return x


def _logf_kernel(gt_ref, bf_ref, lf_ref, c_ref):
    lf = jax.nn.log_sigmoid(gt_ref[0] + bf_ref[...])
    lf_ref[0] = lf
    c_ref[0] = _lane_cumsum(lf)


def fox_logf(gates_t, b_f):
    n_seq, h, t = gates_t.shape
    spec = pl.BlockSpec((1, h, t), lambda s: (s, 0, 0))
    shape = jax.ShapeDtypeStruct((n_seq, h, t), F32)
    return pl.pallas_call(
        _logf_kernel,
        grid=(n_seq,),
        in_specs=[spec, pl.BlockSpec((h, 1), lambda s: (0, 0))],
        out_specs=[spec, spec],
        out_shape=[shape, shape],
        compiler_params=_params(("parallel",), 1 << 20),
        name="fox_logf",
    )(gates_t, b_f.reshape(h, 1))


FLASH_HEADS = 2


def _flash_logits(q, k_ref, ck_ref, cq, hh, j, *, tk, scale):
    ks = pl.multiple_of(j * tk, tk)
    k = k_ref[pl.ds(ks, tk), hh * HEAD:(hh + 1) * HEAD].astype(BF16)
    return _dot_nt(q, k) * scale + (cq - ck_ref[0, hh, :, pl.ds(ks, tk)])


def _flash_update(s, v_ref, hh, carry, j, *, tk, q0, masked):
    m, l, acc = carry
    ks = pl.multiple_of(j * tk, tk)
    v = v_ref[pl.ds(ks, tk), hh * HEAD:(hh + 1) * HEAD].astype(BF16)
    if masked:
        qpos = q0 + lax.broadcasted_iota(jnp.int32, s.shape, 0)
        kpos = ks + lax.broadcasted_iota(jnp.int32, s.shape, 1)
        s = jnp.where(kpos <= qpos, s, -jnp.inf)
    m_new = jnp.maximum(m, jnp.max(s, axis=-1, keepdims=True))
    alpha = jnp.exp(m - m_new)
    p = jnp.exp(s - m_new)
    l = alpha * l + jnp.sum(p, axis=-1, keepdims=True)
    acc = alpha * acc + _dot(p.astype(BF16), v)
    return m_new, l, acc


def _flash_kernel(q_ref, k_ref, v_ref, cq_ref, ck_ref, o_ref, *, scale):
    tq = q_ref.shape[0]
    nh = q_ref.shape[1] // HEAD
    qi = pl.program_id(2)
    logits = [functools.partial(_flash_logits, q_ref[:, hh * HEAD:(hh + 1) * HEAD].astype(BF16), k_ref, ck_ref,
                                cq_ref[0, hh], hh, tk=tq, scale=scale) for hh in range(nh)]
    update = functools.partial(_flash_update, v_ref=v_ref, tk=tq, q0=qi * tq)
    init = (jnp.full((tq, 1), -jnp.inf, F32), jnp.zeros((tq, 1), F32), jnp.zeros((tq, HEAD), F32))

    def body(j, carries):
        return tuple(update(logits[hh](j), hh=hh, carry=c, j=j, masked=False) for hh, c in enumerate(carries))

    carries = lax.fori_loop(0, qi, body, (init,) * nh)
    for hh, c in enumerate(carries):
        m, l, acc = update(logits[hh](qi), hh=hh, carry=c, j=qi, masked=True)
        o_ref[:, hh * HEAD:(hh + 1) * HEAD] = (acc / l).astype(o_ref.dtype)


def fox_prompt_attention(q, k, v, c, *, n_seq):
    m, width = q.shape
    h = width // HEAD
    t = m // n_seq
    tq = min(t, 512)
    nq = t // tq
    cq = c.reshape(n_seq, h, t, 1)
    ck = c.reshape(n_seq, h, 1, t)
    nh = FLASH_HEADS if h % FLASH_HEADS == 0 else 1
    wblk = nh * HEAD
    vmem = (4 * _nbytes((tq, wblk), F32) + 4 * _nbytes((t, wblk), F32) + 2 * nh * _nbytes((tq, LANES), F32)
            + 8 * nh * _nbytes((tq, tq), F32))
    return pl.pallas_call(
        functools.partial(_flash_kernel, scale=HEAD ** -0.5),
        grid=(n_seq, h // nh, nq),
        in_specs=[pl.BlockSpec((tq, wblk), lambda b, hh, i: (b * nq + i, hh)),
                  pl.BlockSpec((t, wblk), lambda b, hh, i: (b, hh)),
                  pl.BlockSpec((t, wblk), lambda b, hh, i: (b, hh)),
                  pl.BlockSpec((1, nh, tq, 1), lambda b, hh, i: (b, hh, i, 0)),
                  pl.BlockSpec((1, nh, 1, t), lambda b, hh, i: (b, hh, 0, 0))],
        out_specs=pl.BlockSpec((tq, wblk), lambda b, hh, i: (b * nq + i, hh)),
        out_shape=jax.ShapeDtypeStruct((m, width), BF16),
        compiler_params=_params(("parallel", "parallel", "arbitrary"), vmem),
        name="fox_prompt_attention",
    )(q, k, v, cq, ck)


def _split3(x):
    hi = x.astype(BF16)
    r = x - hi.astype(F32)
    mid = r.astype(BF16)
    lo = (r - mid.astype(F32)).astype(BF16)
    return hi, mid, lo


def _decode_attend(kget, vget, past_t, causal, qs_ref, cn_ref, s_ref, rep_ref, pv_ref, m_ref, l_ref, acc_ref, *,
                   heads, t, scale):
    keys = past_t.shape[1]
    for hh in range(heads):
        rows = slice(hh * t, (hh + 1) * t)
        s_ref[rows, :] = _dot_nt(qs_ref[rows, :].astype(BF16), kget(hh))
        rep_ref[rows, :] = jnp.broadcast_to(past_t[hh:hh + 1, :], (t, keys))
    s = s_ref[...] * scale + (cn_ref[0] - rep_ref[...])
    if causal:
        qpos = lax.broadcasted_iota(jnp.int32, s.shape, 0) % t
        kpos = lax.broadcasted_iota(jnp.int32, s.shape, 1)
        s = jnp.where(kpos <= qpos, s, -jnp.inf)
    m = m_ref[...]
    m_new = jnp.maximum(m, jnp.max(s, axis=-1, keepdims=True))
    alpha = jnp.exp(m - m_new)
    p = jnp.exp(s - m_new)
    l_ref[...] = alpha * l_ref[...] + jnp.sum(p, axis=-1, keepdims=True)
    m_ref[...] = m_new
    s_ref[...] = p
    for hh in range(heads):
        rows = slice(hh * t, (hh + 1) * t)
        pv_ref[rows, :] = _dot(s_ref[rows, :].astype(BF16), vget(hh))
    acc_ref[...] = alpha * acc_ref[...] + pv_ref[...]


def _decode_kernel(pt_ref, q_ref, kp_ref, vp_ref, lf_ref, kn_ref, vn_ref, cn_ref, cnrow_ref, o_ref,
                   qs_ref, s_ref, rep_ref, pv_ref, m_ref, l_ref, acc_ref, carry_ref, *, heads, t, scale):
    del pt_ref
    p = pl.program_id(1)

    @pl.when(p == 0)
    def _():
        m_ref[...] = jnp.full(m_ref.shape, -jnp.inf, F32)
        l_ref[...] = jnp.zeros(l_ref.shape, F32)
        acc_ref[...] = jnp.zeros(acc_ref.shape, F32)
        carry_ref[...] = jnp.zeros(carry_ref.shape, F32)
        for hh in range(heads):
            qs_ref[hh * t:(hh + 1) * t, :] = q_ref[:, hh * HEAD:(hh + 1) * HEAD]

    attend = functools.partial(_decode_attend, qs_ref=qs_ref, cn_ref=cn_ref, s_ref=s_ref, rep_ref=rep_ref,
                               pv_ref=pv_ref, m_ref=m_ref, l_ref=l_ref, acc_ref=acc_ref,
                               heads=heads, t=t, scale=scale)

    x = lf_ref[...]
    n = x.shape[0]
    row = lax.broadcasted_iota(jnp.int32, (n, n), 0)
    col = lax.broadcasted_iota(jnp.int32, (n, n), 1)
    after = jnp.where(row > col, 1.0, 0.0).astype(BF16)
    ones = jnp.ones((n, n), BF16)
    hi, mid, lo = _split3(x)
    later_t = (_dot_tn(hi, after) + _dot_tn(mid, after)) + _dot_tn(lo, after)
    past_t = -(later_t + carry_ref[...])
    carry_ref[...] = carry_ref[...] + ((_dot_tn(hi, ones) + _dot_tn(mid, ones)) + _dot_tn(lo, ones))
    kt = jnp.swapaxes(kp_ref[...], 0, 1).astype(BF16)
    vt = jnp.swapaxes(vp_ref[...], 0, 1).astype(BF16)
    attend(lambda hh: kt[hh], lambda hh: vt[hh], past_t, False)

    @pl.when(p == pl.num_programs(1) - 1)
    def _():
        attend(lambda hh: kn_ref[:, hh * HEAD:(hh + 1) * HEAD].astype(BF16),
               lambda hh: vn_ref[:, hh * HEAD:(hh + 1) * HEAD].astype(BF16), cnrow_ref[...], True)
        for hh in range(heads):
            rows = slice(hh * t, (hh + 1) * t)
            o_ref[:, hh * HEAD:(hh + 1) * HEAD] = acc_ref[rows, :] / l_ref[rows, :]


def fox_sample_attention(q, k_new, v_new, c_new, cache_k, cache_v, cache_logf, slot, page_table):
    n_seq, n_pages = page_table.shape
    m, width = q.shape
    h = width // HEAD
    t = m // n_seq
    page = cache_k.shape[2]
    pad = lambda a: jnp.pad(a.reshape(n_seq, t, width), ((0, 0), (0, page - t), (0, 0)))
    cn_col = c_new[:, :, :t].reshape(n_seq, h * t, 1)
    cn_row = jnp.pad(c_new[:, :, :t], ((0, 0), (0, 0), (0, page - t)))
    last = n_pages - 1
    grid_spec = pltpu.PrefetchScalarGridSpec(
        num_scalar_prefetch=1,
        grid=(n_seq, n_pages),
        in_specs=[pl.BlockSpec((t, width), lambda b, p, pt: (b, 0)),
                  pl.BlockSpec((None, None, page, h, HEAD), lambda b, p, pt: (slot, pt[b, last - p], 0, 0, 0)),
                  pl.BlockSpec((None, None, page, h, HEAD), lambda b, p, pt: (slot, pt[b, last - p], 0, 0, 0)),
                  pl.BlockSpec((None, None, page, h), lambda b, p, pt: (slot, pt[b, last - p], 0, 0)),
                  pl.BlockSpec((None, page, width), lambda b, p, pt: (b, 0, 0)),
                  pl.BlockSpec((None, page, width), lambda b, p, pt: (b, 0, 0)),
                  pl.BlockSpec((1, h * t, 1), lambda b, p, pt: (b, 0, 0)),
                  pl.BlockSpec((None, h, page), lambda b, p, pt: (b, 0, 0))],
        out_specs=pl.BlockSpec((t, width), lambda b, p, pt: (b, 0)),
        scratch_shapes=[pltpu.VMEM((h * t, HEAD), F32), pltpu.VMEM((h * t, page), F32),
                        pltpu.VMEM((h * t, page), F32), pltpu.VMEM((h * t, HEAD), F32),
                        pltpu.VMEM((h * t, 1), F32), pltpu.VMEM((h * t, 1), F32),
                        pltpu.VMEM((h * t, HEAD), F32), pltpu.VMEM((h, page), F32)],
    )
    vmem = 8 * _nbytes((page, width), F32) + 2 * _nbytes((page, width), BF16)
    return pl.pallas_call(
        functools.partial(_decode_kernel, heads=h, t=t, scale=HEAD ** -0.5),
        grid_spec=grid_spec,
        out_shape=jax.ShapeDtypeStruct((m, width), F32),
        compiler_params=_params(("arbitrary", "arbitrary"), vmem),
        name="fox_sample_attention",
    )(page_table, q, cache_k, cache_v, cache_logf, pad(k_new), pad(v_new), cn_col, cn_row)


def _gdn_gate_kernel(gt_ref, alog_ref, dtb_ref, beta_ref, gc_ref, *, heads, t_valid):
    g_all = gt_ref[0]
    lane = lax.broadcasted_iota(jnp.int32, (heads, g_all.shape[1]), 1)
    valid = lane < t_valid
    beta = jnp.where(valid, jax.nn.sigmoid(g_all[:heads]), 0.0)
    g = -jnp.exp(alog_ref[...]) * jax.nn.softplus(g_all[heads:] + dtb_ref[...])
    g = jnp.where(valid, g, 0.0)
    beta_ref[0] = beta
    gc_ref[0] = _lane_cumsum(g, GDN_CHUNK)


def gdn_gates(gates_t, a_log, dt_bias, *, t_valid):
    n_seq, h2, t = gates_t.shape
    h = h2 // 2
    out_spec = pl.BlockSpec((1, h, t), lambda s: (s, 0, 0))
    shape = jax.ShapeDtypeStruct((n_seq, h, t), F32)
    return pl.pallas_call(
        functools.partial(_gdn_gate_kernel, heads=h, t_valid=t_valid),
        grid=(n_seq,),
        in_specs=[pl.BlockSpec((1, h2, t), lambda s: (s, 0, 0)),
                  pl.BlockSpec((h, 1), lambda s: (0, 0)),
                  pl.BlockSpec((h, 1), lambda s: (0, 0))],
        out_specs=[out_spec, out_spec],
        out_shape=[shape, shape],
        compiler_params=_params(("parallel",), 1 << 20),
        name="gdn_gates",
    )(gates_t, a_log.reshape(h, 1), dt_bias.reshape(h, 1))


def _row_to_col(row, eye):
    return jnp.sum(jnp.where(eye, row, 0.0), axis=1, keepdims=True)


def _gdn_kernel(q_ref, k_ref, v_ref, z_ref, beta_ref, gc_ref, s0_ref, ng_ref, o_ref, s_ref,
                u_s, wq_s, kd_s, in_s, *, hb, n_chunks):
    c_len = GDN_CHUNK
    n = hb * c_len
    row = lax.broadcasted_iota(jnp.int32, (n, n), 0)
    col = lax.broadcasted_iota(jnp.int32, (n, n), 1)
    eye = row == col
    same_head = (row // c_len) == (col // c_len)
    lower = jnp.logical_and(same_head, col <= row)
    strict = jnp.logical_and(same_head, col < row)
    ng = ng_ref[...]
    heads = range(hb)
    stack = lambda f: jnp.concatenate([f(hh) for hh in heads], axis=0)
    blk = lambda a, hh: a[hh * c_len:(hh + 1) * c_len, hh * HEAD:(hh + 1) * HEAD]

    def last_of(gc_row, hh):
        return gc_row[:, (hh + 1) * c_len - 1:(hh + 1) * c_len]

    def prepare(c, carry):
        r0 = pl.multiple_of(c * c_len, c_len)

        def unit(ref, hh, scale):
            a = ref[pl.ds(r0, c_len), hh * HEAD:(hh + 1) * HEAD]
            return a * lax.rsqrt(jnp.sum(a * a, axis=-1, keepdims=True) + EPS) * scale

        qn = stack(lambda hh: unit(q_ref, hh, HEAD ** -0.5))
        kn = stack(lambda hh: unit(k_ref, hh, 1.0))
        v = stack(lambda hh: v_ref[pl.ds(r0, c_len), hh * HEAD:(hh + 1) * HEAD])
        beta_row = beta_ref[0, 0, pl.ds(c, 1), :]
        gc_row = gc_ref[0, 0, pl.ds(c, 1), :]
        beta_col = _row_to_col(beta_row, eye)
        gc_col = _row_to_col(gc_row, eye)
        gc_last_col = stack(lambda hh: jnp.broadcast_to(last_of(gc_row, hh), (c_len, 1)))
        decay = jnp.exp(jnp.where(lower, gc_col - gc_row, -jnp.inf))
        kn16 = kn.astype(BF16)
        lmat = jnp.where(strict, _dot_nt(kn16, kn16) * decay * beta_col, 0.0)
        inv = jnp.where(eye, 1.0, 0.0) - lmat
        power = lmat
        span = 2
        while span < c_len:
            p16 = power.astype(BF16)
            power = _dot(p16, p16)
            inv = inv + _dot(inv.astype(BF16), power.astype(BF16))
            span *= 2
        e_col = jnp.exp(gc_col)
        rhs = jnp.concatenate([v * beta_col, kn * (beta_col * e_col)], axis=1)
        sol = _dot(inv.astype(BF16), rhs.astype(BF16))
        u_s[c] = sol[:, :HEAD]
        wq_s[c, 0:n, :] = sol[:, HEAD:].astype(BF16)
        wq_s[c, n:2 * n, :] = (qn * e_col).astype(BF16)
        in_s[c] = (_dot_nt(qn.astype(BF16), kn16) * decay).astype(BF16)
        kd_s[c] = (kn * jnp.exp(gc_last_col - gc_col)).astype(BF16)
        return carry

    lax.fori_loop(0, n_chunks, prepare, 0, unroll=2 if n_chunks % 2 == 0 else 1)

    head_of_row = lax.broadcasted_iota(jnp.int32, (n, HEAD), 0) // c_len

    def advance(c, s_wide):
        r0 = pl.multiple_of(c * c_len, c_len)
        r = _dot(wq_s[c], s_wide.astype(BF16))
        w_s = stack(lambda hh: blk(r, hh))
        q_s = stack(lambda hh: blk(r[n:], hh))
        v_new = u_s[c] - w_s
        o = q_s + _dot(in_s[c], v_new.astype(BF16))
        v_wide = jnp.concatenate([jnp.where(head_of_row == hh, v_new, 0.0) for hh in heads], axis=1).astype(BF16)
        gc_row = gc_ref[0, 0, pl.ds(c, 1), :]
        g_last = jnp.concatenate([jnp.broadcast_to(jnp.exp(last_of(gc_row, hh)), (1, HEAD)) for hh in heads], axis=1)
        s_next = s_wide * g_last + _dot_tn(kd_s[c], v_wide)
        for hh in heads:
            cols = slice(hh * HEAD, (hh + 1) * HEAD)
            oh = o[hh * c_len:(hh + 1) * c_len]
            on = oh * lax.rsqrt(jnp.mean(oh * oh, axis=-1, keepdims=True) + EPS) * ng
            z = z_ref[pl.ds(r0, c_len), cols]
            o_ref[pl.ds(r0, c_len), cols] = (on * _silu(z)).astype(o_ref.dtype)
        return s_next

    s_wide = lax.fori_loop(0, n_chunks, advance, jnp.concatenate([s0_ref[0, hh] for hh in heads], axis=1))
    for hh in heads:
        s_ref[0, hh] = s_wide[:, hh * HEAD:(hh + 1) * HEAD]


def gdn_mix(qkv, z, z_col_off, beta, gc, s0, norm_g, *, n_seq, out_dtype):
    m = qkv.shape[0]
    h = qkv.shape[1] // (3 * HEAD)
    t = m // n_seq
    n_chunks = t // GDN_CHUNK
    hb = 4
    assert h % hb == 0
    nhb = h // hb
    wblk = hb * HEAD
    zoff = z_col_off // wblk
    n = hb * GDN_CHUNK
    by_block = lambda a: jnp.transpose(a.reshape(n_seq, nhb, hb, n_chunks, GDN_CHUNK), (0, 1, 3, 2, 4)).reshape(
        n_seq, nhb, n_chunks, n)
    beta, gc = by_block(beta), by_block(gc)
    seq_spec = lambda off: pl.BlockSpec((t, wblk), lambda b, j: (b, j + off))
    gate_spec = pl.BlockSpec((1, 1, n_chunks, n), lambda b, j: (b, j, 0, 0))
    state_spec = pl.BlockSpec((1, hb, HEAD, HEAD), lambda b, j: (b, j, 0, 0))
    scratch = [pltpu.VMEM((n_chunks, n, HEAD), F32), pltpu.VMEM((n_chunks, 2 * n, HEAD), BF16),
               pltpu.VMEM((n_chunks, n, HEAD), BF16), pltpu.VMEM((n_chunks, n, n), BF16)]
    vmem = (2 * 4 * _nbytes((t, wblk), F32) + 2 * _nbytes((t, wblk), out_dtype)
            + _nbytes((n_chunks, n, HEAD), F32) + _nbytes((n_chunks, 3 * n + 2 * n, HEAD), BF16))
    return pl.pallas_call(
        functools.partial(_gdn_kernel, hb=hb, n_chunks=n_chunks),
        grid=(n_seq, nhb),
        in_specs=[seq_spec(0), seq_spec(nhb), seq_spec(2 * nhb), seq_spec(zoff),
                  gate_spec, gate_spec, state_spec,
                  pl.BlockSpec((1, HEAD), lambda b, j: (0, 0))],
        out_specs=[pl.BlockSpec((t, wblk), lambda b, j: (b, j)), state_spec],
        out_shape=[jax.ShapeDtypeStruct((m, h * HEAD), out_dtype),
                   jax.ShapeDtypeStruct((n_seq, h, HEAD, HEAD), F32)],
        scratch_shapes=scratch,
        compiler_params=_params(("parallel", "parallel"), vmem),
        name="gdn_mix",
    )(qkv, qkv, qkv, z, beta, gc, s0, norm_g.reshape(1, HEAD))


def _xattn_q_kernel(x_ref, g_ref, w_ref, mk_ref, mv_ref, o_ref, wb_ref, *, heads, scale):
    @pl.when(pl.program_id(0) == 0)
    def _():
        wb_ref[...] = w_ref[...].astype(BF16)

    q = _dot(_rmsnorm_rows(x_ref[...], g_ref[...]).astype(BF16), wb_ref[...])
    for hh in range(heads):
        cols = slice(hh * HEAD, (hh + 1) * HEAD)
        s = _dot_nt(q[:, cols].astype(BF16), mk_ref[0, :, cols].astype(BF16)) * scale
        p = jnp.exp(s - jnp.max(s, axis=-1, keepdims=True))
        o = _dot(p.astype(BF16), mv_ref[0, :, cols].astype(BF16)) / jnp.sum(p, axis=-1, keepdims=True)
        o_ref[:, cols] = o.astype(o_ref.dtype)


def xattn_heads(x, g, w_q, mem_k, mem_v, layer, *, n_seq):
    m, d = x.shape
    width = w_q.shape[2]
    t = m // n_seq
    tm = min(t, _row_tile(m, x.dtype, d))
    tps = t // tm
    mem = mem_k.shape[2]
    vmem = (3 * _nbytes((tm, d), F32) + 3 * _nbytes((d, width), F32) + 4 * _nbytes((mem, width), F32)
            + 2 * _nbytes((tm, width), F32) + 8 * _nbytes((tm, mem), F32))
    return pl.pallas_call(
        functools.partial(_xattn_q_kernel, heads=width // HEAD, scale=HEAD ** -0.5),
        grid=(m // tm,),
        in_specs=[pl.BlockSpec((tm, d), lambda i: (i, 0)),
                  pl.BlockSpec((1, d), lambda i: (0, 0)),
                  pl.BlockSpec((None, d, width), lambda i: (layer, 0, 0)),
                  pl.BlockSpec((None, 1, mem, width), lambda i: (layer, i // tps, 0, 0)),
                  pl.BlockSpec((None, 1, mem, width), lambda i: (layer, i // tps, 0, 0))],
        out_specs=pl.BlockSpec((tm, width), lambda i: (i, 0)),
        out_shape=jax.ShapeDtypeStruct((m, width), F32),
        scratch_shapes=[pltpu.VMEM((d, width), BF16)],
        compiler_params=_params(("arbitrary",), vmem),
        name="xattn_heads",
    )(x, g.reshape(1, d), w_q, mem_k, mem_v)


def _xattn_o_kernel(a_ref, w_ref, res_ref, g_ref, x_ref, h_ref, wb_ref):
    @pl.when(pl.program_id(0) == 0)
    def _():
        wb_ref[...] = w_ref[...].astype(BF16)

    x = res_ref[...] + _dot(a_ref[...].astype(BF16), wb_ref[...])
    x_ref[...] = x
    h_ref[...] = _rmsnorm_rows(x, g_ref[...]).astype(h_ref.dtype)


def xattn_out(a, w_o, layer, res, g, h_dtype):
    m, width = a.shape
    d = w_o.shape[2]
    tm = min(m, 256)
    vmem = (2 * _nbytes((tm, width), F32) + 3 * _nbytes((width, d), F32) + 4 * _nbytes((tm, d), F32)
            + 2 * _nbytes((tm, d), h_dtype) + 2 * _nbytes((tm, d), F32))
    return pl.pallas_call(
        _xattn_o_kernel,
        grid=(m // tm,),
        in_specs=[pl.BlockSpec((tm, width), lambda i: (i, 0)),
                  pl.BlockSpec((None, width, d), lambda i: (layer, 0, 0)),
                  pl.BlockSpec((tm, d), lambda i: (i, 0)),
                  pl.BlockSpec((1, d), lambda i: (0, 0))],
        out_specs=[pl.BlockSpec((tm, d), lambda i: (i, 0)), pl.BlockSpec((tm, d), lambda i: (i, 0))],
        out_shape=[jax.ShapeDtypeStruct((m, d), F32), jax.ShapeDtypeStruct((m, d), h_dtype)],
        scratch_shapes=[pltpu.VMEM((width, d), BF16)],
        compiler_params=_params(("arbitrary",), vmem),
        name="xattn_out",
    )(a, w_o, res, g.reshape(1, d))


def _pad_lanes(a, mult):
    pad = (-a.shape[-1]) % mult
    return a if pad == 0 else jnp.pad(a, [(0, 0)] * (a.ndim - 1) + [(0, pad)])


def _trunk(xp, bp, xs, bs, mem_p, mem_s, states, W):
    d = xp.shape[1]
    tp, ts = xp.shape[0] // bp, xs.shape[0] // bs
    depth = W['norm_mix'].shape[0]
    ffn_dim = W['ffn_w_dw'].shape[2]
    heads = d // HEAD
    hv = heads * HEAD
    outs_p = dict(conf=[], k=[], v=[], lf=[], s=[], gbuf=[], ffn=[])
    outs_s = dict(conf=[], k=[], v=[], lf=[], s=[], gbuf=[], ffn=[])
    fox_w_in = W['fox_w_in'].astype(BF16)
    gdn_w_in = W['gdn_w_in'].astype(BF16)

    def logf(gate, n_seq, t, slot):
        gates_t = _pad_lanes(jnp.transpose(gate.reshape(n_seq, t, heads), (0, 2, 1)), LANES)
        lf_t, c_t = fox_logf(gates_t, W['fox_b_f'][slot])
        return jnp.transpose(lf_t[:, :, :t], (0, 2, 1)), c_t

    def gdn_group(qkv, z, gate, s0, n_seq, t, t_pad, slot, out_dtype):
        gates_t = jnp.transpose(gate.reshape(n_seq, t, 2 * heads), (0, 2, 1))
        gates_t = jnp.pad(gates_t, ((0, 0), (0, 0), (0, max(t_pad, LANES) - t)))
        beta, gc = gdn_gates(gates_t, W['gdn_a_log'][slot], W['gdn_dt_bias'][slot], t_valid=t)
        return gdn_mix(qkv, z, 0, beta[:, :, :t_pad], gc[:, :, :t_pad], s0, W['gdn_norm_g'][slot],
                       n_seq=n_seq, out_dtype=out_dtype)

    for i in range(depth):
        kind, slot = i % 3, i // 3
        hp = rmsnorm(xp, W['norm_mix'][i], BF16)
        hs = rmsnorm(xs, W['norm_mix'][i], F32)
        if kind == 0:
            up, us = matmul_glu(hp, W['conf_w_pw1'], slot, hs)
            taps = W['conf_w_dw'].shape[1]
            outs_p['conf'].append(up.reshape(bp, tp, d)[:, tp - (taps - 1):])
            buf = states['conf'][slot]
            halo = jnp.pad(buf, ((0, 0), (CONF_HALO - (taps - 1), 0), (0, 0))).reshape(bs * CONF_HALO, d)
            outs_s['conf'].append(_next_state(buf, us, bs, d))
            conv = functools.partial(conformer_conv, w_dw=W['conf_w_dw'][slot], ln_g=W['conf_ln_g'][slot],
                                     ln_b=W['conf_ln_b'][slot])
            yp = conv(up, None, n_seq=bp, out_dtype=BF16)
            ys = conv(us, halo, n_seq=bs, out_dtype=F32)
            xp, xs = matmul(yp, W['conf_w_pw2'], slot, res=xp, x2=ys, res2=xs)
        elif kind == 1:
            w_in = fox_w_in
            qp, qs = matmul(hp, w_in, slot, col_off=0, n_out=hv, x2=hs)
            kp, ks = matmul(hp, w_in, slot, col_off=hv, n_out=hv, x2=hs)
            vp, vs = matmul(hp, w_in, slot, col_off=2 * hv, n_out=hv, x2=hs)
            gate_p, gate_s = matmul(hp, w_in[slot, :, 3 * hv:], x2=hs)
            lf_p, c_p = logf(gate_p, bp, tp, slot)
            lf_s, c_s = logf(gate_s, bs, ts, slot)
            op = fox_prompt_attention(qp, kp, vp, c_p, n_seq=bp)
            cache_k, cache_v, cache_lf, page_table = states['fox']
            os_ = fox_sample_attention(qs, ks, vs, c_s, cache_k, cache_v, cache_lf, slot, page_table)
            xp, xs = matmul(op, W['fox_w_o'], slot, res=xp, x2=os_, res2=xs)
            for outs, k, v, lf, n_seq, t in ((outs_p, kp, vp, lf_p, bp, tp), (outs_s, ks, vs, lf_s, bs, ts)):
                outs['k'].append(k.reshape(n_seq, t, heads, HEAD))
                outs['v'].append(v.reshape(n_seq, t, heads, HEAD))
                outs['lf'].append(lf)
        else:
            w_in = gdn_w_in
            cw = W['gdn_w_conv']
            n_qkv = cw.shape[2]
            qkv_p, tails, pre_s = matmul_conv(hp, w_in, cw, slot, hs, n_seq=bp, n_out=n_qkv, gated=False,
                                              out_dtype=F32)
            outs_p['gbuf'].append(tails[:, SUBLANES - (cw.shape[1] - 1):])
            zp, zs = matmul(hp, w_in, slot, col_off=n_qkv, n_out=hv, x2=hs)
            gate_p, gate_s = matmul(hp, w_in[slot, :, n_qkv + hv:], x2=hs)
            op, s_p = gdn_group(qkv_p, zp, gate_p, jnp.zeros((bp, heads, HEAD, HEAD), F32), bp, tp, tp, slot, BF16)
            buf = states['gdn_conv'][slot]
            qkv_s = state_conv(pre_s, None, buf, cw[slot])
            outs_s['gbuf'].append(_next_state(buf, pre_s, bs, n_qkv))
            ts_pad = -(-ts // GDN_CHUNK) * GDN_CHUNK
            rows = lambda a: jnp.pad(a.reshape(bs, ts, -1), ((0, 0), (0, ts_pad - ts), (0, 0))).reshape(
                bs * ts_pad, -1)
            os_, s_s = gdn_group(rows(qkv_s), rows(zs), gate_s, states['gdn'][slot], bs, ts, ts_pad, slot, F32)
            os_ = os_.reshape(bs, ts_pad, hv)[:, :ts].reshape(bs * ts, hv)
            xp, xs = matmul(op, W['gdn_w_o'], slot, res=xp, x2=os_, res2=xs)
            outs_p['s'].append(s_p)
            outs_s['s'].append(s_s)
        ap = xattn_heads(xp, W['norm_mem'][i], W['x_w_q'], mem_p[0], mem_p[1], i, n_seq=bp)
        as_ = xattn_heads(xs, W['norm_mem'][i], W['x_w_q'], mem_s[0], mem_s[1], i, n_seq=bs)
        xp, hp = xattn_out(ap, W['x_w_o'], i, xp, W['norm_ffn'][i], BF16)
        xs, hs = xattn_out(as_, W['x_w_o'], i, xs, W['norm_ffn'][i], F32)
        cw = W['ffn_w_dw']
        yp, tails, pre_g, pre_v = matmul_conv(hp, W['ffn_w_up'], cw, i, hs, n_seq=bp, n_out=ffn_dim, gated=True,
                                              out_dtype=BF16)
        outs_p['ffn'].append(tails[:, SUBLANES - (cw.shape[1] - 1):])
        buf = states['ffn_conv'][i]
        ys = state_conv(pre_g, pre_v, buf, cw[i])
        outs_s['ffn'].append(_next_state(buf, pre_g, bs, ffn_dim))
        xp, xs = matmul(yp, W['ffn_w_down'], i, res=xp, x2=ys, res2=xs)
    finish = lambda x, n_seq, t, outs: (
        rmsnorm(x, W['norm_final'], F32).reshape(n_seq, t, d),
        *(jnp.stack(outs[name]) for name in ('conf', 'k', 'v', 'lf', 's', 'gbuf', 'ffn')))
    return finish(xp, bp, tp, outs_p), finish(xs, bs, ts, outs_s)


def kernel(x_prompt, x_sample, cache_fox_k, cache_fox_v, cache_fox_logf, cache_mem_k, cache_mem_v, state_conf,
           state_gdn, state_gdn_conv, state_ffn_conv, page_table, mem_prompt, norm_mix, norm_mem, norm_ffn,
           norm_final, conf_w_pw1, conf_w_dw, conf_ln_g, conf_ln_b, conf_w_pw2, fox_w_in, fox_b_f, fox_w_o,
           gdn_w_in, gdn_w_conv, gdn_a_log, gdn_dt_bias, gdn_norm_g, gdn_w_o, x_w_q, x_w_kv, x_w_o, ffn_w_up,
           ffn_w_dw, ffn_w_down):
    W = dict(norm_mix=norm_mix, norm_mem=norm_mem, norm_ffn=norm_ffn, norm_final=norm_final,
             conf_w_pw1=conf_w_pw1, conf_w_dw=conf_w_dw, conf_ln_g=conf_ln_g, conf_ln_b=conf_ln_b,
             conf_w_pw2=conf_w_pw2, fox_w_in=fox_w_in, fox_b_f=fox_b_f, fox_w_o=fox_w_o,
             gdn_w_in=gdn_w_in, gdn_w_conv=gdn_w_conv, gdn_a_log=gdn_a_log, gdn_dt_bias=gdn_dt_bias,
             gdn_norm_g=gdn_norm_g, gdn_w_o=gdn_w_o, x_w_q=x_w_q, x_w_o=x_w_o,
             ffn_w_up=ffn_w_up, ffn_w_dw=ffn_w_dw, ffn_w_down=ffn_w_down)
    b, t, d = x_prompt.shape
    bs, ts, _ = x_sample.shape
    depth = x_w_kv.shape[0]
    mem = mem_prompt.shape[1]
    xw = x_w_kv.shape[2] // 2
    xh = xw // HEAD

    mem_rows = mem_prompt.reshape(b * mem, d)
    kv = jnp.stack([matmul(mem_rows, x_w_kv, i) for i in range(depth)])
    mem_k_p = kv[..., :xw].reshape(depth, b, mem, xw)
    mem_v_p = kv[..., xw:].reshape(depth, b, mem, xw)

    states = dict(conf=state_conf, gdn=state_gdn, gdn_conv=state_gdn_conv, ffn_conv=state_ffn_conv,
                  fox=(cache_fox_k, cache_fox_v, cache_fox_logf, page_table))
    mem_s = (cache_mem_k.reshape(depth, bs, mem, xw), cache_mem_v.reshape(depth, bs, mem, xw))
    ((y_p, conf_p, fox_k_p, fox_v_p, fox_lf_p, gdn_p, gdn_conv_p, ffn_conv_p),
     (y_s, conf_s, fox_k_s, fox_v_s, fox_lf_s, gdn_s, gdn_conv_s, ffn_conv_s)) = _trunk(
        x_prompt.reshape(b * t, d), b, x_sample.reshape(bs * ts, d), bs, (mem_k_p, mem_v_p), mem_s, states, W)

    return (y_p, y_s,
            conf_p, fox_k_p, fox_v_p, fox_lf_p, gdn_p, gdn_conv_p,
            mem_k_p.reshape(depth, b, mem, xh, HEAD), mem_v_p.reshape(depth, b, mem, xh, HEAD), ffn_conv_p,
            conf_s, fox_k_s, fox_v_s, fox_lf_s, gdn_s, gdn_conv_s, ffn_conv_s)
```

```python
import functools

import jax
import jax.numpy as jnp
from jax import lax
from jax.experimental import pallas as pl
from jax.experimental.pallas import tpu as pltpu

EPS = 1e-6
HEAD = 128
GDN_CHUNK = 64
SUBLANES = 8
LANES = 128
VMEM_CAP = 58 * 1024 * 1024
BF16 = jnp.bfloat16
F32 = jnp.float32


def _params(semantics, vmem_bytes):
    limit = int(min(VMEM_CAP, max(vmem_bytes * 5 // 4 + (4 << 20), 16 << 20)))
    return pltpu.CompilerParams(dimension_semantics=semantics, vmem_limit_bytes=limit)


def _nbytes(shape, dtype):
    n = 1
    for s in shape:
        n *= s
    return n * jnp.dtype(dtype).itemsize


def _silu(x):
    return x * jax.nn.sigmoid(x)


def _dot(a, b):
    return jnp.dot(a, b, preferred_element_type=F32)


def _dot_nt(a, b):
    return lax.dot_general(a, b, (((1,), (1,)), ((), ())), preferred_element_type=F32)


def _dot_tn(a, b):
    return lax.dot_general(a, b, (((0,), (0,)), ((), ())), preferred_element_type=F32)


def _rmsnorm_rows(x, g):
    return x * lax.rsqrt(jnp.mean(x * x, axis=-1, keepdims=True) + EPS) * g


def _rmsnorm_kernel(x_ref, g_ref, o_ref):
    o_ref[...] = _rmsnorm_rows(x_ref[...], g_ref[...]).astype(o_ref.dtype)


def rmsnorm(x, g, out_dtype):
    m, d = x.shape
    tm = min(m, 256)
    vmem = 2 * (_nbytes((tm, d), F32) + _nbytes((tm, d), out_dtype))
    return pl.pallas_call(
        _rmsnorm_kernel,
        grid=(m // tm,),
        in_specs=[pl.BlockSpec((tm, d), lambda i: (i, 0)),
                  pl.BlockSpec((1, d), lambda i: (0, 0))],
        out_specs=pl.BlockSpec((tm, d), lambda i: (i, 0)),
        out_shape=jax.ShapeDtypeStruct((m, d), out_dtype),
        compiler_params=_params(("parallel",), vmem),
        name="rmsnorm",
    )(x, g.reshape(1, d))


def _row_tile(m, x_dtype, k):
    cap = 1024 if jnp.dtype(x_dtype).itemsize == 2 else 512
    if k > 4096:
        cap //= 4
    tm = min(m, cap)
    assert m % tm == 0
    return tm


def _col_tile(n, cap):
    tn = min(n, cap)
    assert n % tn == 0
    return tn


def _mm_kernel(*refs, has_res, has_x2):
    refs = list(refs)
    x_ref, w_ref = refs[:2]
    del refs[:2]
    res_ref = refs.pop(0) if has_res else None
    x2_ref = refs.pop(0) if has_x2 else None
    res2_ref = refs.pop(0) if has_x2 and has_res else None
    o_ref = refs.pop(0)
    o2_ref = refs.pop(0) if has_x2 else None
    wb_ref, = refs

    @pl.when(pl.program_id(1) == 0)
    def _():
        wb_ref[...] = w_ref[...].astype(BF16)
        if has_x2:
            acc2 = _dot(x2_ref[...].astype(BF16), wb_ref[...])
            if has_res:
                acc2 = res2_ref[...] + acc2
            o2_ref[...] = acc2

    acc = _dot(x_ref[...].astype(BF16), wb_ref[...])
    if has_res:
        acc = res_ref[...] + acc
    o_ref[...] = acc.astype(o_ref.dtype)


def _stacked(w, layer):
    return (w[None], 0) if w.ndim == 2 else (w, layer)


def _x2_spec(x2):
    return pl.BlockSpec(x2.shape, lambda n, i: (0, 0), pipeline_mode=pl.Buffered(1))


def matmul(x, w, layer=0, *, col_off=0, n_out=None, res=None, out_dtype=F32, x2=None, res2=None):
    w, layer = _stacked(w, layer)
    m, k = x.shape
    n_out = w.shape[2] - col_off if n_out is None else n_out
    deep = k > 4096
    tn = _col_tile(n_out, 512)
    assert col_off % tn == 0
    off = col_off // tn
    tm = _row_tile(m, x.dtype, k)
    w_bufs = 1 if deep else 2
    w_mode = dict(pipeline_mode=pl.Buffered(1)) if deep else {}
    tile = pl.BlockSpec((tm, tn), lambda n, i: (i, n))
    in_specs = [pl.BlockSpec((tm, k), lambda n, i: (i, 0)),
                pl.BlockSpec((None, k, tn), lambda n, i: (layer, 0, n + off), **w_mode)]
    args = [x, w]
    out_specs = [tile]
    out_shape = [jax.ShapeDtypeStruct((m, n_out), out_dtype)]
    vmem = (2 * _nbytes((tm, k), x.dtype) + w_bufs * _nbytes((k, tn), F32) + _nbytes((k, tn), BF16)
            + 2 * _nbytes((tm, tn), out_dtype))
    if res is not None:
        in_specs.append(tile)
        args.append(res)
        vmem += 2 * _nbytes((tm, tn), F32)
    if x2 is not None:
        m2 = x2.shape[0]
        tile2 = pl.BlockSpec((m2, tn), lambda n, i: (0, n))
        in_specs.append(_x2_spec(x2))
        args.append(x2)
        if res is not None:
            in_specs.append(tile2)
            args.append(res2)
        out_specs.append(tile2)
        out_shape.append(jax.ShapeDtypeStruct((m2, n_out), F32))
        vmem += _nbytes(x2.shape, x2.dtype) + 4 * _nbytes((m2, tn), F32)
    out = pl.pallas_call(
        functools.partial(_mm_kernel, has_res=res is not None, has_x2=x2 is not None),
        grid=(n_out // tn, m // tm),
        in_specs=in_specs,
        out_specs=out_specs,
        out_shape=out_shape,
        scratch_shapes=[pltpu.VMEM((k, tn), BF16)],
        compiler_params=_params(("arbitrary", "arbitrary"), vmem),
        name="matmul",
    )(*args)
    return out[0] if x2 is None else out


def _glu_kernel(x_ref, wa_ref, wb_ref, x2_ref, o_ref, o2_ref, wab_ref, wbb_ref):
    glu = lambda x: _dot(x, wab_ref[...]) * jax.nn.sigmoid(_dot(x, wbb_ref[...]))

    @pl.when(pl.program_id(1) == 0)
    def _():
        wab_ref[...] = wa_ref[...].astype(BF16)
        wbb_ref[...] = wb_ref[...].astype(BF16)
        o2_ref[...] = glu(x2_ref[...].astype(BF16))

    o_ref[...] = glu(x_ref[...].astype(BF16))


def matmul_glu(x, w, layer, x2):
    m, k = x.shape
    m2 = x2.shape[0]
    n = w.shape[2] // 2
    tn = _col_tile(n, 256)
    tm = _row_tile(m, x.dtype, k)
    nb = n // tn
    vmem = (2 * _nbytes((tm, k), x.dtype) + 4 * _nbytes((k, tn), F32) + 2 * _nbytes((k, tn), BF16)
            + 2 * _nbytes((tm, tn), F32) + _nbytes(x2.shape, x2.dtype) + 2 * _nbytes((m2, tn), F32))
    return pl.pallas_call(
        _glu_kernel,
        grid=(nb, m // tm),
        in_specs=[pl.BlockSpec((tm, k), lambda j, i: (i, 0)),
                  pl.BlockSpec((None, k, tn), lambda j, i: (layer, 0, j)),
                  pl.BlockSpec((None, k, tn), lambda j, i: (layer, 0, j + nb)),
                  _x2_spec(x2)],
        out_specs=[pl.BlockSpec((tm, tn), lambda j, i: (i, j)), pl.BlockSpec((m2, tn), lambda j, i: (0, j))],
        out_shape=[jax.ShapeDtypeStruct((m, n), F32), jax.ShapeDtypeStruct((m2, n), F32)],
        scratch_shapes=[pltpu.VMEM((k, tn), BF16), pltpu.VMEM((k, tn), BF16)],
        compiler_params=_params(("arbitrary", "arbitrary"), vmem),
        name="matmul_glu",
    )(x, w, w, x2)


def _mm_conv_kernel(*refs, width, tiles_per_seq, gated):
    if gated:
        x_ref, wg_ref, wv_ref, cw_ref, x2_ref, y_ref, tail_ref, g2_ref, v2_ref, wgb_ref, wvb_ref, ext_ref = refs
    else:
        x_ref, wg_ref, cw_ref, x2_ref, y_ref, tail_ref, g2_ref, wgb_ref, ext_ref = refs
    i = pl.program_id(1)

    @pl.when(i == 0)
    def _():
        x2 = x2_ref[...].astype(BF16)
        wgb_ref[...] = wg_ref[...].astype(BF16)
        g2_ref[...] = _dot(x2, wgb_ref[...])
        if gated:
            wvb_ref[...] = wv_ref[...].astype(BF16)
            v2_ref[...] = _dot(x2, wvb_ref[...])

    @pl.when(i % tiles_per_seq == 0)
    def _():
        ext_ref[0:SUBLANES, :] = jnp.zeros((SUBLANES, ext_ref.shape[1]), F32)

    x = x_ref[...].astype(BF16)
    g = _dot(x, wgb_ref[...])
    tm = g.shape[0]
    ext_ref[SUBLANES:SUBLANES + tm, :] = g
    cw = cw_ref[...]
    y = g * cw[width - 1:width, :]
    for j in range(width - 1):
        y = y + ext_ref[pl.ds(SUBLANES - (width - 1) + j, tm), :] * cw[j:j + 1, :]
    y = _silu(y)
    if gated:
        y = y * _dot(x, wvb_ref[...])
    y_ref[...] = y.astype(y_ref.dtype)
    tail = g[tm - SUBLANES:tm, :]
    tail_ref[0] = tail
    ext_ref[0:SUBLANES, :] = tail


def matmul_conv(x, w, cw, layer, x2, *, n_seq, n_out, gated, out_dtype):
    m, k = x.shape
    m2 = x2.shape[0]
    width = cw.shape[1]
    tn = _col_tile(n_out, 256 if gated else 512)
    t = m // n_seq
    tm = _row_tile(t, x.dtype, k)
    tps = t // tm
    nb = n_out // tn
    in_specs = [pl.BlockSpec((tm, k), lambda j, i: (i, 0)),
                pl.BlockSpec((None, k, tn), lambda j, i: (layer, 0, j))]
    args = [x, w]
    scratch = [pltpu.VMEM((k, tn), BF16)]
    if gated:
        in_specs.append(pl.BlockSpec((None, k, tn), lambda j, i: (layer, 0, j + nb)))
        args.append(w)
        scratch.append(pltpu.VMEM((k, tn), BF16))
    in_specs.append(pl.BlockSpec((None, width, tn), lambda j, i: (layer, 0, j)))
    args.append(cw)
    in_specs.append(_x2_spec(x2))
    args.append(x2)
    scratch.append(pltpu.VMEM((tm + SUBLANES, tn), F32))
    nw = 2 if gated else 1
    vmem = (2 * _nbytes((tm, k), x.dtype) + nw * (2 * _nbytes((k, tn), F32) + _nbytes((k, tn), BF16))
            + 2 * _nbytes((tm, tn), out_dtype) + 6 * _nbytes((tm, tn), F32)
            + _nbytes(x2.shape, x2.dtype) + 2 * nw * _nbytes((m2, tn), F32))
    tile2 = pl.BlockSpec((m2, tn), lambda j, i: (0, j))
    plain2 = jax.ShapeDtypeStruct((m2, n_out), F32)
    return pl.pallas_call(
        functools.partial(_mm_conv_kernel, width=width, tiles_per_seq=tps, gated=gated),
        grid=(nb, m // tm),
        in_specs=in_specs,
        out_specs=[pl.BlockSpec((tm, tn), lambda j, i: (i, j)),
                   pl.BlockSpec((1, SUBLANES, tn), lambda j, i: (i // tps, 0, j))] + [tile2] * nw,
        out_shape=[jax.ShapeDtypeStruct((m, n_out), out_dtype),
                   jax.ShapeDtypeStruct((n_seq, SUBLANES, n_out), F32)] + [plain2] * nw,
        scratch_shapes=scratch,
        compiler_params=_params(("arbitrary", "arbitrary"), vmem),
        name="matmul_conv",
    )(*args)


def _state_conv_kernel(*refs, width, gated):
    if gated:
        u_ref, v_ref, st_ref, cw_ref, y_ref, ext_ref = refs
    else:
        u_ref, st_ref, cw_ref, y_ref, ext_ref = refs
    t = u_ref.shape[0]
    ext_ref[SUBLANES - (width - 1):SUBLANES, :] = st_ref[0]
    ext_ref[SUBLANES:SUBLANES + t, :] = u_ref[...]
    cw = cw_ref[...]
    y = jnp.zeros(u_ref.shape, F32)
    for j in range(width):
        y = y + ext_ref[pl.ds(SUBLANES - (width - 1) + j, t), :] * cw[j:j + 1, :]
    y = _silu(y)
    if gated:
        y = y * v_ref[...]
    y_ref[...] = y


def state_conv(u, v, state, cw):
    n_seq, wm1, n_out = state.shape
    t = u.shape[0] // n_seq
    width = wm1 + 1
    gated = v is not None
    tc = _col_tile(n_out, 256)
    nb = n_out // tc
    in_specs = [pl.BlockSpec((t, tc), lambda s, j: (s, j))]
    args = [u]
    if gated:
        in_specs.append(pl.BlockSpec((t, tc), lambda s, j: (s, j)))
        args.append(v)
    in_specs += [pl.BlockSpec((1, wm1, tc), lambda s, j: (s, 0, j)),
                 pl.BlockSpec((width, tc), lambda s, j: (0, j))]
    args += [state, cw]
    return pl.pallas_call(
        functools.partial(_state_conv_kernel, width=width, gated=gated),
        grid=(n_seq, nb),
        in_specs=in_specs,
        out_specs=pl.BlockSpec((t, tc), lambda s, j: (s, j)),
        out_shape=jax.ShapeDtypeStruct((n_seq * t, n_out), F32),
        scratch_shapes=[pltpu.VMEM((SUBLANES + t, tc), F32)],
        compiler_params=_params(("parallel", "parallel"), 1 << 20),
        name="state_conv",
    )(*args)


def _next_state(state, u, n_seq, n_out):
    wm1 = state.shape[1]
    ext = jnp.concatenate([state, u[:, :n_out].reshape(n_seq, -1, n_out)], axis=1)
    return ext[:, ext.shape[1] - wm1:]


CONF_HALO = 32
CONF_ROWS = 64
CONF_COLS = 256


def _conf_kernel(u_ref, halo_ref, w_ref, lg_ref, lb_ref, o_ref, ext_ref, y_ref, sh_ref, *,
                 taps, tiles_per_seq, halo_is_prev_rows):
    tr, d = u_ref.shape
    if halo_is_prev_rows:
        first = pl.program_id(0) % tiles_per_seq == 0

        @pl.when(first)
        def _():
            ext_ref[0:CONF_HALO, :] = jnp.zeros((CONF_HALO, d), F32)

        @pl.when(jnp.logical_not(first))
        def _():
            ext_ref[0:CONF_HALO, :] = halo_ref[...]
    else:
        ext_ref[0:CONF_HALO, :] = halo_ref[...]
    ext_ref[CONF_HALO:CONF_HALO + tr, :] = u_ref[...]

    rows = min(tr, CONF_ROWS)
    lead = CONF_HALO - (taps - 1)

    def col_block(c, carry):
        c0 = pl.multiple_of(c * CONF_COLS, CONF_COLS)
        w = w_ref[:, pl.ds(c0, CONF_COLS)]
        for r in range(min(SUBLANES, taps)):
            span = tr + (len(range(r, taps, SUBLANES)) - 1) * SUBLANES
            sh_ref[r, 0:span, :] = ext_ref[pl.ds(lead + r, span), pl.ds(c0, CONF_COLS)]
        for r0 in range(0, tr, rows):
            acc = jnp.zeros((rows, CONF_COLS), F32)
            for j in range(taps):
                r = j % SUBLANES
                acc = acc + sh_ref[r, r0 + j - r:r0 + j - r + rows, :] * w[j:j + 1, :]
            y_ref[r0:r0 + rows, pl.ds(c0, CONF_COLS)] = acc
        return carry

    lax.fori_loop(0, d // CONF_COLS, col_block, 0)

    def row_block(i, carry):
        r0 = pl.multiple_of(i * rows, rows)
        y = y_ref[pl.ds(r0, rows), :]
        yc = y - jnp.mean(y, axis=-1, keepdims=True)
        yn = yc * lax.rsqrt(jnp.mean(yc * yc, axis=-1, keepdims=True) + EPS)
        yn = yn * lg_ref[...] + lb_ref[...]
        o_ref[pl.ds(r0, rows), :] = _silu(yn).astype(o_ref.dtype)
        return carry

    lax.fori_loop(0, tr // rows, row_block, 0)


def conformer_conv(u, halo, w_dw, ln_g, ln_b, *, n_seq, out_dtype):
    m, d = u.shape
    taps = w_dw.shape[0]
    assert taps - 1 <= CONF_HALO and d % CONF_COLS == 0
    t = m // n_seq
    tr = min(t, 256)
    assert t % tr == 0 and tr % SUBLANES == 0
    tps = t // tr
    prev = halo is None
    if prev:
        assert tr % CONF_HALO == 0
        per = tr // CONF_HALO
        halo_arr = u
        halo_spec = pl.BlockSpec((CONF_HALO, d), lambda i: (jnp.maximum(i * per - 1, 0), 0))
    else:
        assert tps == 1
        halo_arr = halo
        halo_spec = pl.BlockSpec((CONF_HALO, d), lambda i: (i, 0))
    vmem = (2 * _nbytes((tr, d), F32) + 2 * _nbytes((CONF_HALO, d), F32) + 2 * _nbytes((tr, d), out_dtype)
            + _nbytes((2 * tr + CONF_HALO, d), F32) + 4 * _nbytes((32, d), F32))
    return pl.pallas_call(
        functools.partial(_conf_kernel, taps=taps, tiles_per_seq=tps, halo_is_prev_rows=prev),
        grid=(m // tr,),
        in_specs=[pl.BlockSpec((tr, d), lambda i: (i, 0)),
                  halo_spec,
                  pl.BlockSpec((taps, d), lambda i: (0, 0)),
                  pl.BlockSpec((1, d), lambda i: (0, 0)),
                  pl.BlockSpec((1, d), lambda i: (0, 0))],
        out_specs=pl.BlockSpec((tr, d), lambda i: (i, 0)),
        out_shape=jax.ShapeDtypeStruct((m, d), out_dtype),
        scratch_shapes=[pltpu.VMEM((CONF_HALO + tr, d), F32), pltpu.VMEM((tr, d), F32),
                        pltpu.VMEM((SUBLANES, tr + CONF_HALO, CONF_COLS), F32)],
        compiler_params=_params(("arbitrary",), vmem),
        name="conformer_conv",
    )(u, halo_arr, w_dw, ln_g.reshape(1, d), ln_b.reshape(1, d))


def _lane_cumsum(x, seg=None):
    n = x.shape[-1]
    lane = lax.broadcasted_iota(jnp.int32, x.shape, x.ndim - 1)
    pos = lane if seg is None else lane % seg
    d = 1
    while d < (n if seg is None else seg):
        x = x + jnp.where(pos >= d, pltpu.roll(x, d, x.ndim - 1), 0.0)
        d *= 2
    return x


def _logf_kernel(gt_ref, bf_ref, lf_ref, c_ref):
    lf = jax.nn.log_sigmoid(gt_ref[0] + bf_ref[...])
    lf_ref[0] = lf
    c_ref[0] = _lane_cumsum(lf)


def fox_logf(gates_t, b_f):
    n_seq, h, t = gates_t.shape
    spec = pl.BlockSpec((1, h, t), lambda s: (s, 0, 0))
    shape = jax.ShapeDtypeStruct((n_seq, h, t), F32)
    return pl.pallas_call(
        _logf_kernel,
        grid=(n_seq,),
        in_specs=[spec, pl.BlockSpec((h, 1), lambda s: (0, 0))],
        out_specs=[spec, spec],
        out_shape=[shape, shape],
        compiler_params=_params(("parallel",), 1 << 20),
        name="fox_logf",
    )(gates_t, b_f.reshape(h, 1))


FLASH_HEADS = 2


def _flash_logits(q, k_ref, ck_ref, cq, hh, j, *, tk, scale):
    ks = pl.multiple_of(j * tk, tk)
    k = k_ref[pl.ds(ks, tk), hh * HEAD:(hh + 1) * HEAD].astype(BF16)
    return _dot_nt(q, k) * scale + (cq - ck_ref[0, hh, :, pl.ds(ks, tk)])


def _flash_update(s, v_ref, hh, carry, j, *, tk, q0, masked):
    m, l, acc = carry
    ks = pl.multiple_of(j * tk, tk)
    v = v_ref[pl.ds(ks, tk), hh * HEAD:(hh + 1) * HEAD].astype(BF16)
    if masked:
        qpos = q0 + lax.broadcasted_iota(jnp.int32, s.shape, 0)
        kpos = ks + lax.broadcasted_iota(jnp.int32, s.shape, 1)
        s = jnp.where(kpos <= qpos, s, -jnp.inf)
    m_new = jnp.maximum(m, jnp.max(s, axis=-1, keepdims=True))
    alpha = jnp.exp(m - m_new)
    p = jnp.exp(s - m_new)
    l = alpha * l + jnp.sum(p, axis=-1, keepdims=True)
    acc = alpha * acc + _dot(p.astype(BF16), v)
    return m_new, l, acc


def _flash_kernel(q_ref, k_ref, v_ref, cq_ref, ck_ref, o_ref, *, scale):
    tq = q_ref.shape[0]
    nh = q_ref.shape[1] // HEAD
    qi = pl.program_id(2)
    logits = [functools.partial(_flash_logits, q_ref[:, hh * HEAD:(hh + 1) * HEAD].astype(BF16), k_ref, ck_ref,
                                cq_ref[0, hh], hh, tk=tq, scale=scale) for hh in range(nh)]
    update = functools.partial(_flash_update, v_ref=v_ref, tk=tq, q0=qi * tq)
    init = (jnp.full((tq, 1), -jnp.inf, F32), jnp.zeros((tq, 1), F32), jnp.zeros((tq, HEAD), F32))

    def body(j, carries):
        return tuple(update(logits[hh](j), hh=hh, carry=c, j=j, masked=False) for hh, c in enumerate(carries))

    carries = lax.fori_loop(0, qi, body, (init,) * nh)
    for hh, c in enumerate(carries):
        m, l, acc = update(logits[hh](qi), hh=hh, carry=c, j=qi, masked=True)
        o_ref[:, hh * HEAD:(hh + 1) * HEAD] = (acc / l).astype(o_ref.dtype)


def fox_prompt_attention(q, k, v, c, *, n_seq):
    m, width = q.shape
    h = width // HEAD
    t = m // n_seq
    tq = min(t, 512)
    nq = t // tq
    cq = c.reshape(n_seq, h, t, 1)
    ck = c.reshape(n_seq, h, 1, t)
    nh = FLASH_HEADS if h % FLASH_HEADS == 0 else 1
    wblk = nh * HEAD
    vmem = (4 * _nbytes((tq, wblk), F32) + 4 * _nbytes((t, wblk), F32) + 2 * nh * _nbytes((tq, LANES), F32)
            + 8 * nh * _nbytes((tq, tq), F32))
    return pl.pallas_call(
        functools.partial(_flash_kernel, scale=HEAD ** -0.5),
        grid=(n_seq, h // nh, nq),
        in_specs=[pl.BlockSpec((tq, wblk), lambda b, hh, i: (b * nq + i, hh)),
                  pl.BlockSpec((t, wblk), lambda b, hh, i: (b, hh)),
                  pl.BlockSpec((t, wblk), lambda b, hh, i: (b, hh)),
                  pl.BlockSpec((1, nh, tq, 1), lambda b, hh, i: (b, hh, i, 0)),
                  pl.BlockSpec((1, nh, 1, t), lambda b, hh, i: (b, hh, 0, 0))],
        out_specs=pl.BlockSpec((tq, wblk), lambda b, hh, i: (b * nq + i, hh)),
        out_shape=jax.ShapeDtypeStruct((m, width), BF16),
        compiler_params=_params(("parallel", "parallel", "arbitrary"), vmem),
        name="fox_prompt_attention",
    )(q, k, v, cq, ck)


def _split3(x):
    hi = x.astype(BF16)
    r = x - hi.astype(F32)
    mid = r.astype(BF16)
    lo = (r - mid.astype(F32)).astype(BF16)
    return hi, mid, lo


def _decode_attend(kget, vget, past_t, causal, qs_ref, cn_ref, s_ref, rep_ref, pv_ref, m_ref, l_ref, acc_ref, *,
                   heads, t, scale):
    keys = past_t.shape[1]
    for hh in range(heads):
        rows = slice(hh * t, (hh + 1) * t)
        s_ref[rows, :] = _dot_nt(qs_ref[rows, :].astype(BF16), kget(hh))
        rep_ref[rows, :] = jnp.broadcast_to(past_t[hh:hh + 1, :], (t, keys))
    s = s_ref[...] * scale + (cn_ref[0] - rep_ref[...])
    if causal:
        qpos = lax.broadcasted_iota(jnp.int32, s.shape, 0) % t
        kpos = lax.broadcasted_iota(jnp.int32, s.shape, 1)
        s = jnp.where(kpos <= qpos, s, -jnp.inf)
    m = m_ref[...]
    m_new = jnp.maximum(m, jnp.max(s, axis=-1, keepdims=True))
    alpha = jnp.exp(m - m_new)
    p = jnp.exp(s - m_new)
    l_ref[...] = alpha * l_ref[...] + jnp.sum(p, axis=-1, keepdims=True)
    m_ref[...] = m_new
    s_ref[...] = p
    for hh in range(heads):
        rows = slice(hh * t, (hh + 1) * t)
        pv_ref[rows, :] = _dot(s_ref[rows, :].astype(BF16), vget(hh))
    acc_ref[...] = alpha * acc_ref[...] + pv_ref[...]


def _decode_kernel(pt_ref, q_ref, kp_ref, vp_ref, lf_ref, kn_ref, vn_ref, cn_ref, cnrow_ref, o_ref,
                   qs_ref, s_ref, rep_ref, pv_ref, m_ref, l_ref, acc_ref, carry_ref, *, heads, t, scale):
    del pt_ref
    p = pl.program_id(1)

    @pl.when(p == 0)
    def _():
        m_ref[...] = jnp.full(m_ref.shape, -jnp.inf, F32)
        l_ref[...] = jnp.zeros(l_ref.shape, F32)
        acc_ref[...] = jnp.zeros(acc_ref.shape, F32)
        carry_ref[...] = jnp.zeros(carry_ref.shape, F32)
        for hh in range(heads):
            qs_ref[hh * t:(hh + 1) * t, :] = q_ref[:, hh * HEAD:(hh + 1) * HEAD]

    attend = functools.partial(_decode_attend, qs_ref=qs_ref, cn_ref=cn_ref, s_ref=s_ref, rep_ref=rep_ref,
                               pv_ref=pv_ref, m_ref=m_ref, l_ref=l_ref, acc_ref=acc_ref,
                               heads=heads, t=t, scale=scale)

    x = lf_ref[...]
    n = x.shape[0]
    row = lax.broadcasted_iota(jnp.int32, (n, n), 0)
    col = lax.broadcasted_iota(jnp.int32, (n, n), 1)
    after = jnp.where(row > col, 1.0, 0.0).astype(BF16)
    ones = jnp.ones((n, n), BF16)
    hi, mid, lo = _split3(x)
    later_t = (_dot_tn(hi, after) + _dot_tn(mid, after)) + _dot_tn(lo, after)
    past_t = -(later_t + carry_ref[...])
    carry_ref[...] = carry_ref[...] + ((_dot_tn(hi, ones) + _dot_tn(mid, ones)) + _dot_tn(lo, ones))
    kt = jnp.swapaxes(kp_ref[...].astype(BF16), 0, 1)
    vt = jnp.swapaxes(vp_ref[...].astype(BF16), 0, 1)
    attend(lambda hh: kt[hh], lambda hh: vt[hh], past_t, False)

    @pl.when(p == pl.num_programs(1) - 1)
    def _():
        attend(lambda hh: kn_ref[:, hh * HEAD:(hh + 1) * HEAD].astype(BF16),
               lambda hh: vn_ref[:, hh * HEAD:(hh + 1) * HEAD].astype(BF16), cnrow_ref[...], True)
        for hh in range(heads):
            rows = slice(hh * t, (hh + 1) * t)
            o_ref[:, hh * HEAD:(hh + 1) * HEAD] = acc_ref[rows, :] / l_ref[rows, :]


def fox_sample_attention(q, k_new, v_new, c_new, cache_k, cache_v, cache_logf, slot, page_table):
    n_seq, n_pages = page_table.shape
    m, width = q.shape
    h = width // HEAD
    t = m // n_seq
    page = cache_k.shape[2]
    pad = lambda a: jnp.pad(a.reshape(n_seq, t, width), ((0, 0), (0, page - t), (0, 0)))
    cn_col = c_new[:, :, :t].reshape(n_seq, h * t, 1)
    cn_row = jnp.pad(c_new[:, :, :t], ((0, 0), (0, 0), (0, page - t)))
    last = n_pages - 1
    grid_spec = pltpu.PrefetchScalarGridSpec(
        num_scalar_prefetch=1,
        grid=(n_seq, n_pages),
        in_specs=[pl.BlockSpec((t, width), lambda b, p, pt: (b, 0)),
                  pl.BlockSpec((None, None, page, h, HEAD), lambda b, p, pt: (slot, pt[b, last - p], 0, 0, 0)),
                  pl.BlockSpec((None, None, page, h, HEAD), lambda b, p, pt: (slot, pt[b, last - p], 0, 0, 0)),
                  pl.BlockSpec((None, None, page, h), lambda b, p, pt: (slot, pt[b, last - p], 0, 0)),
                  pl.BlockSpec((None, page, width), lambda b, p, pt: (b, 0, 0)),
                  pl.BlockSpec((None, page, width), lambda b, p, pt: (b, 0, 0)),
                  pl.BlockSpec((1, h * t, 1), lambda b, p, pt: (b, 0, 0)),
                  pl.BlockSpec((None, h, page), lambda b, p, pt: (b, 0, 0))],
        out_specs=pl.BlockSpec((t, width), lambda b, p, pt: (b, 0)),
        scratch_shapes=[pltpu.VMEM((h * t, HEAD), F32), pltpu.VMEM((h * t, page), F32),
                        pltpu.VMEM((h * t, page), F32), pltpu.VMEM((h * t, HEAD), F32),
                        pltpu.VMEM((h * t, 1), F32), pltpu.VMEM((h * t, 1), F32),
                        pltpu.VMEM((h * t, HEAD), F32), pltpu.VMEM((h, page), F32)],
    )
    vmem = 8 * _nbytes((page, width), F32) + 2 * _nbytes((page, width), BF16)
    return pl.pallas_call(
        functools.partial(_decode_kernel, heads=h, t=t, scale=HEAD ** -0.5),
        grid_spec=grid_spec,
        out_shape=jax.ShapeDtypeStruct((m, width), F32),
        compiler_params=_params(("arbitrary", "arbitrary"), vmem),
        name="fox_sample_attention",
    )(page_table, q, cache_k, cache_v, cache_logf, pad(k_new), pad(v_new), cn_col, cn_row)


def _gdn_gate_kernel(gt_ref, alog_ref, dtb_ref, beta_ref, gc_ref, *, heads, t_valid):
    g_all = gt_ref[0]
    lane = lax.broadcasted_iota(jnp.int32, (heads, g_all.shape[1]), 1)
    valid = lane < t_valid
    beta = jnp.where(valid, jax.nn.sigmoid(g_all[:heads]), 0.0)
    g = -jnp.exp(alog_ref[...]) * jax.nn.softplus(g_all[heads:] + dtb_ref[...])
    g = jnp.where(valid, g, 0.0)
    beta_ref[0] = beta
    gc_ref[0] = _lane_cumsum(g, GDN_CHUNK)


def gdn_gates(gates_t, a_log, dt_bias, *, t_valid):
    n_seq, h2, t = gates_t.shape
    h = h2 // 2
    out_spec = pl.BlockSpec((1, h, t), lambda s: (s, 0, 0))
    shape = jax.ShapeDtypeStruct((n_seq, h, t), F32)
    return pl.pallas_call(
        functools.partial(_gdn_gate_kernel, heads=h, t_valid=t_valid),
        grid=(n_seq,),
        in_specs=[pl.BlockSpec((1, h2, t), lambda s: (s, 0, 0)),
                  pl.BlockSpec((h, 1), lambda s: (0, 0)),
                  pl.BlockSpec((h, 1), lambda s: (0, 0))],
        out_specs=[out_spec, out_spec],
        out_shape=[shape, shape],
        compiler_params=_params(("parallel",), 1 << 20),
        name="gdn_gates",
    )(gates_t, a_log.reshape(h, 1), dt_bias.reshape(h, 1))


def _row_to_col(row, eye):
    return jnp.sum(jnp.where(eye, row, 0.0), axis=1, keepdims=True)


def _gdn_kernel(q_ref, k_ref, v_ref, z_ref, beta_ref, gc_ref, s0_ref, ng_ref, o_ref, s_ref,
                u_s, wq_s, kd_s, in_s, *, hb, n_chunks):
    c_len = GDN_CHUNK
    n = hb * c_len
    row = lax.broadcasted_iota(jnp.int32, (n, n), 0)
    col = lax.broadcasted_iota(jnp.int32, (n, n), 1)
    eye = row == col
    same_head = (row // c_len) == (col // c_len)
    lower = jnp.logical_and(same_head, col <= row)
    strict = jnp.logical_and(same_head, col < row)
    ng = ng_ref[...]
    heads = range(hb)
    stack = lambda f: jnp.concatenate([f(hh) for hh in heads], axis=0)
    blk = lambda a, hh: a[hh * c_len:(hh + 1) * c_len, hh * HEAD:(hh + 1) * HEAD]

    def last_of(gc_row, hh):
        return gc_row[:, (hh + 1) * c_len - 1:(hh + 1) * c_len]

    def prepare(c, carry):
        r0 = pl.multiple_of(c * c_len, c_len)

        def unit(ref, hh, scale):
            a = ref[pl.ds(r0, c_len), hh * HEAD:(hh + 1) * HEAD]
            return a * lax.rsqrt(jnp.sum(a * a, axis=-1, keepdims=True) + EPS) * scale

        qn = stack(lambda hh: unit(q_ref, hh, HEAD ** -0.5))
        kn = stack(lambda hh: unit(k_ref, hh, 1.0))
        v = stack(lambda hh: v_ref[pl.ds(r0, c_len), hh * HEAD:(hh + 1) * HEAD])
        beta_row = beta_ref[0, 0, pl.ds(c, 1), :]
        gc_row = gc_ref[0, 0, pl.ds(c, 1), :]
        beta_col = _row_to_col(beta_row, eye)
        gc_col = _row_to_col(gc_row, eye)
        gc_last_col = stack(lambda hh: jnp.broadcast_to(last_of(gc_row, hh), (c_len, 1)))
        decay = jnp.exp(jnp.where(lower, gc_col - gc_row, -jnp.inf))
        kn16 = kn.astype(BF16)
        kk_qk = _dot_nt(jnp.concatenate([kn16, qn.astype(BF16)], axis=0), kn16)
        lmat = jnp.where(strict, kk_qk[:n] * decay * beta_col, 0.0)
        inv = jnp.where(eye, 1.0, 0.0) - lmat
        p16 = lmat.astype(BF16)
        p16 = _dot(p16, p16).astype(BF16)
        span = 4
        while span < c_len:
            both = _dot(jnp.concatenate([p16, inv.astype(BF16)], axis=0), p16)
            p16 = both[:n].astype(BF16)
            inv = inv + both[n:]
            span *= 2
        inv = inv + _dot(inv.astype(BF16), p16)
        e_col = jnp.exp(gc_col)
        rhs = jnp.concatenate([v * beta_col, kn * (beta_col * e_col)], axis=1)
        sol = _dot(inv.astype(BF16), rhs.astype(BF16))
        u_s[c] = sol[:, :HEAD]
        wq_s[c, 0:n, :] = sol[:, HEAD:].astype(BF16)
        wq_s[c, n:2 * n, :] = (qn * e_col).astype(BF16)
        in_s[c] = (kk_qk[n:] * decay).astype(BF16)
        kd_s[c] = (kn * jnp.exp(gc_last_col - gc_col)).astype(BF16)
        return carry

    lax.fori_loop(0, n_chunks, prepare, 0, unroll=2 if n_chunks % 2 == 0 else 1)

    head_of_row = lax.broadcasted_iota(jnp.int32, (n, HEAD), 0) // c_len

    def advance(c, s_wide):
        r0 = pl.multiple_of(c * c_len, c_len)
        r = _dot(wq_s[c], s_wide.astype(BF16))
        w_s = stack(lambda hh: blk(r, hh))
        q_s = stack(lambda hh: blk(r[n:], hh))
        v_new = u_s[c] - w_s
        o = q_s + _dot(in_s[c], v_new.astype(BF16))
        v_wide = jnp.concatenate([jnp.where(head_of_row == hh, v_new, 0.0) for hh in heads], axis=1).astype(BF16)
        gc_row = gc_ref[0, 0, pl.ds(c, 1), :]
        g_last = jnp.concatenate([jnp.broadcast_to(jnp.exp(last_of(gc_row, hh)), (1, HEAD)) for hh in heads], axis=1)
        s_next = s_wide * g_last + _dot_tn(kd_s[c], v_wide)
        for hh in heads:
            cols = slice(hh * HEAD, (hh + 1) * HEAD)
            oh = o[hh * c_len:(hh + 1) * c_len]
            on = oh * lax.rsqrt(jnp.mean(oh * oh, axis=-1, keepdims=True) + EPS) * ng
            z = z_ref[pl.ds(r0, c_len), cols]
            o_ref[pl.ds(r0, c_len), cols] = (on * _silu(z)).astype(o_ref.dtype)
        return s_next

    s_wide = lax.fori_loop(0, n_chunks, advance, jnp.concatenate([s0_ref[0, hh] for hh in heads], axis=1))
    for hh in heads:
        s_ref[0, hh] = s_wide[:, hh * HEAD:(hh + 1) * HEAD]


def gdn_mix(qkv, z, z_col_off, beta, gc, s0, norm_g, *, n_seq, out_dtype):
    m = qkv.shape[0]
    h = qkv.shape[1] // (3 * HEAD)
    t = m // n_seq
    n_chunks = t // GDN_CHUNK
    hb = 4
    assert h % hb == 0
    nhb = h // hb
    wblk = hb * HEAD
    zoff = z_col_off // wblk
    n = hb * GDN_CHUNK
    by_block = lambda a: jnp.transpose(a.reshape(n_seq, nhb, hb, n_chunks, GDN_CHUNK), (0, 1, 3, 2, 4)).reshape(
        n_seq, nhb, n_chunks, n)
    beta, gc = by_block(beta), by_block(gc)
    seq_spec = lambda off: pl.BlockSpec((t, wblk), lambda b, j: (b, j + off))
    gate_spec = pl.BlockSpec((1, 1, n_chunks, n), lambda b, j: (b, j, 0, 0))
    state_spec = pl.BlockSpec((1, hb, HEAD, HEAD), lambda b, j: (b, j, 0, 0))
    scratch = [pltpu.VMEM((n_chunks, n, HEAD), F32), pltpu.VMEM((n_chunks, 2 * n, HEAD), BF16),
               pltpu.VMEM((n_chunks, n, HEAD), BF16), pltpu.VMEM((n_chunks, n, n), BF16)]
    vmem = (2 * 4 * _nbytes((t, wblk), F32) + 2 * _nbytes((t, wblk), out_dtype)
            + _nbytes((n_chunks, n, HEAD), F32) + _nbytes((n_chunks, 3 * n + 2 * n, HEAD), BF16))
    return pl.pallas_call(
        functools.partial(_gdn_kernel, hb=hb, n_chunks=n_chunks),
        grid=(n_seq, nhb),
        in_specs=[seq_spec(0), seq_spec(nhb), seq_spec(2 * nhb), seq_spec(zoff),
                  gate_spec, gate_spec, state_spec,
                  pl.BlockSpec((1, HEAD), lambda b, j: (0, 0))],
        out_specs=[pl.BlockSpec((t, wblk), lambda b, j: (b, j)), state_spec],
        out_shape=[jax.ShapeDtypeStruct((m, h * HEAD), out_dtype),
                   jax.ShapeDtypeStruct((n_seq, h, HEAD, HEAD), F32)],
        scratch_shapes=scratch,
        compiler_params=_params(("parallel", "parallel"), vmem),
        name="gdn_mix",
    )(qkv, qkv, qkv, z, beta, gc, s0, norm_g.reshape(1, HEAD))


def _xattn_q_kernel(x_ref, g_ref, w_ref, mk_ref, mv_ref, o_ref, wb_ref, *, heads, scale):
    @pl.when(pl.program_id(0) == 0)
    def _():
        wb_ref[...] = w_ref[...].astype(BF16)

    q = _dot(_rmsnorm_rows(x_ref[...], g_ref[...]).astype(BF16), wb_ref[...])
    for hh in range(heads):
        cols = slice(hh * HEAD, (hh + 1) * HEAD)
        s = _dot_nt(q[:, cols].astype(BF16), mk_ref[0, :, cols].astype(BF16)) * scale
        p = jnp.exp(s - jnp.max(s, axis=-1, keepdims=True))
        o = _dot(p.astype(BF16), mv_ref[0, :, cols].astype(BF16)) / jnp.sum(p, axis=-1, keepdims=True)
        o_ref[:, cols] = o.astype(o_ref.dtype)


def xattn_heads(x, g, w_q, mem_k, mem_v, layer, *, n_seq):
    m, d = x.shape
    width = w_q.shape[2]
    t = m // n_seq
    tm = min(t, _row_tile(m, x.dtype, d))
    tps = t // tm
    mem = mem_k.shape[2]
    vmem = (3 * _nbytes((tm, d), F32) + 3 * _nbytes((d, width), F32) + 4 * _nbytes((mem, width), F32)
            + 2 * _nbytes((tm, width), F32) + 8 * _nbytes((tm, mem), F32))
    return pl.pallas_call(
        functools.partial(_xattn_q_kernel, heads=width // HEAD, scale=HEAD ** -0.5),
        grid=(m // tm,),
        in_specs=[pl.BlockSpec((tm, d), lambda i: (i, 0)),
                  pl.BlockSpec((1, d), lambda i: (0, 0)),
                  pl.BlockSpec((None, d, width), lambda i: (layer, 0, 0)),
                  pl.BlockSpec((None, 1, mem, width), lambda i: (layer, i // tps, 0, 0)),
                  pl.BlockSpec((None, 1, mem, width), lambda i: (layer, i // tps, 0, 0))],
        out_specs=pl.BlockSpec((tm, width), lambda i: (i, 0)),
        out_shape=jax.ShapeDtypeStruct((m, width), F32),
        scratch_shapes=[pltpu.VMEM((d, width), BF16)],
        compiler_params=_params(("arbitrary",), vmem),
        name="xattn_heads",
    )(x, g.reshape(1, d), w_q, mem_k, mem_v)


def _xattn_o_kernel(a_ref, w_ref, res_ref, g_ref, x_ref, h_ref, wb_ref):
    @pl.when(pl.program_id(0) == 0)
    def _():
        wb_ref[...] = w_ref[...].astype(BF16)

    x = res_ref[...] + _dot(a_ref[...].astype(BF16), wb_ref[...])
    x_ref[...] = x
    h_ref[...] = _rmsnorm_rows(x, g_ref[...]).astype(h_ref.dtype)


def xattn_out(a, w_o, layer, res, g, h_dtype):
    m, width = a.shape
    d = w_o.shape[2]
    tm = min(m, 256)
    vmem = (2 * _nbytes((tm, width), F32) + 3 * _nbytes((width, d), F32) + 4 * _nbytes((tm, d), F32)
            + 2 * _nbytes((tm, d), h_dtype) + 2 * _nbytes((tm, d), F32))
    return pl.pallas_call(
        _xattn_o_kernel,
        grid=(m // tm,),
        in_specs=[pl.BlockSpec((tm, width), lambda i: (i, 0)),
                  pl.BlockSpec((None, width, d), lambda i: (layer, 0, 0)),
                  pl.BlockSpec((tm, d), lambda i: (i, 0)),
                  pl.BlockSpec((1, d), lambda i: (0, 0))],
        out_specs=[pl.BlockSpec((tm, d), lambda i: (i, 0)), pl.BlockSpec((tm, d), lambda i: (i, 0))],
        out_shape=[jax.ShapeDtypeStruct((m, d), F32), jax.ShapeDtypeStruct((m, d), h_dtype)],
        scratch_shapes=[pltpu.VMEM((width, d), BF16)],
        compiler_params=_params(("arbitrary",), vmem),
        name="xattn_out",
    )(a, w_o, res, g.reshape(1, d))


def _pad_lanes(a, mult):
    pad = (-a.shape[-1]) % mult
    return a if pad == 0 else jnp.pad(a, [(0, 0)] * (a.ndim - 1) + [(0, pad)])


def _trunk(xp, bp, xs, bs, mem_p, mem_s, states, W):
    d = xp.shape[1]
    tp, ts = xp.shape[0] // bp, xs.shape[0] // bs
    depth = W['norm_mix'].shape[0]
    ffn_dim = W['ffn_w_dw'].shape[2]
    heads = d // HEAD
    hv = heads * HEAD
    outs_p = dict(conf=[], k=[], v=[], lf=[], s=[], gbuf=[], ffn=[])
    outs_s = dict(conf=[], k=[], v=[], lf=[], s=[], gbuf=[], ffn=[])
    fox_w_in = W['fox_w_in'].astype(BF16)
    gdn_w_in = W['gdn_w_in'].astype(BF16)

    def logf(gate, n_seq, t, slot):
        gates_t = _pad_lanes(jnp.transpose(gate.reshape(n_seq, t, heads), (0, 2, 1)), LANES)
        lf_t, c_t = fox_logf(gates_t, W['fox_b_f'][slot])
        return jnp.transpose(lf_t[:, :, :t], (0, 2, 1)), c_t

    def gdn_group(qkv, z, gate, s0, n_seq, t, t_pad, slot, out_dtype):
        gates_t = jnp.transpose(gate.reshape(n_seq, t, 2 * heads), (0, 2, 1))
        gates_t = jnp.pad(gates_t, ((0, 0), (0, 0), (0, max(t_pad, LANES) - t)))
        beta, gc = gdn_gates(gates_t, W['gdn_a_log'][slot], W['gdn_dt_bias'][slot], t_valid=t)
        return gdn_mix(qkv, z, 0, beta[:, :, :t_pad], gc[:, :, :t_pad], s0, W['gdn_norm_g'][slot],
                       n_seq=n_seq, out_dtype=out_dtype)

    for i in range(depth):
        kind, slot = i % 3, i // 3
        hp = rmsnorm(xp, W['norm_mix'][i], BF16)
        hs = rmsnorm(xs, W['norm_mix'][i], F32)
        if kind == 0:
            up, us = matmul_glu(hp, W['conf_w_pw1'], slot, hs)
            taps = W['conf_w_dw'].shape[1]
            outs_p['conf'].append(up.reshape(bp, tp, d)[:, tp - (taps - 1):])
            buf = states['conf'][slot]
            halo = jnp.pad(buf, ((0, 0), (CONF_HALO - (taps - 1), 0), (0, 0))).reshape(bs * CONF_HALO, d)
            outs_s['conf'].append(_next_state(buf, us, bs, d))
            conv = functools.partial(conformer_conv, w_dw=W['conf_w_dw'][slot], ln_g=W['conf_ln_g'][slot],
                                     ln_b=W['conf_ln_b'][slot])
            yp = conv(up, None, n_seq=bp, out_dtype=BF16)
            ys = conv(us, halo, n_seq=bs, out_dtype=F32)
            xp, xs = matmul(yp, W['conf_w_pw2'], slot, res=xp, x2=ys, res2=xs)
        elif kind == 1:
            w_in = fox_w_in
            qp, qs = matmul(hp, w_in, slot, col_off=0, n_out=hv, x2=hs)
            kp, ks = matmul(hp, w_in, slot, col_off=hv, n_out=hv, x2=hs)
            vp, vs = matmul(hp, w_in, slot, col_off=2 * hv, n_out=hv, x2=hs)
            gate_p, gate_s = matmul(hp, w_in[slot, :, 3 * hv:], x2=hs)
            lf_p, c_p = logf(gate_p, bp, tp, slot)
            lf_s, c_s = logf(gate_s, bs, ts, slot)
            op = fox_prompt_attention(qp, kp, vp, c_p, n_seq=bp)
            cache_k, cache_v, cache_lf, page_table = states['fox']
            os_ = fox_sample_attention(qs, ks, vs, c_s, cache_k, cache_v, cache_lf, slot, page_table)
            xp, xs = matmul(op, W['fox_w_o'], slot, res=xp, x2=os_, res2=xs)
            for outs, k, v, lf, n_seq, t in ((outs_p, kp, vp, lf_p, bp, tp), (outs_s, ks, vs, lf_s, bs, ts)):
                outs['k'].append(k.reshape(n_seq, t, heads, HEAD))
                outs['v'].append(v.reshape(n_seq, t, heads, HEAD))
                outs['lf'].append(lf)
        else:
            w_in = gdn_w_in
            cw = W['gdn_w_conv']
            n_qkv = cw.shape[2]
            qkv_p, tails, pre_s = matmul_conv(hp, w_in, cw, slot, hs, n_seq=bp, n_out=n_qkv, gated=False,
                                              out_dtype=F32)
            outs_p['gbuf'].append(tails[:, SUBLANES - (cw.shape[1] - 1):])
            zp, zs = matmul(hp, w_in, slot, col_off=n_qkv, n_out=hv, x2=hs)
            gate_p, gate_s = matmul(hp, w_in[slot, :, n_qkv + hv:], x2=hs)
            op, s_p = gdn_group(qkv_p, zp, gate_p, jnp.zeros((bp, heads, HEAD, HEAD), F32), bp, tp, tp, slot, BF16)
            buf = states['gdn_conv'][slot]
            qkv_s = state_conv(pre_s, None, buf, cw[slot])
            outs_s['gbuf'].append(_next_state(buf, pre_s, bs, n_qkv))
            ts_pad = -(-ts // GDN_CHUNK) * GDN_CHUNK
            rows = lambda a: jnp.pad(a.reshape(bs, ts, -1), ((0, 0), (0, ts_pad - ts), (0, 0))).reshape(
                bs * ts_pad, -1)
            os_, s_s = gdn_group(rows(qkv_s), rows(zs), gate_s, states['gdn'][slot], bs, ts, ts_pad, slot, F32)
            os_ = os_.reshape(bs, ts_pad, hv)[:, :ts].reshape(bs * ts, hv)
            xp, xs = matmul(op, W['gdn_w_o'], slot, res=xp, x2=os_, res2=xs)
            outs_p['s'].append(s_p)
            outs_s['s'].append(s_s)
        ap = xattn_heads(xp, W['norm_mem'][i], W['x_w_q'], mem_p[0], mem_p[1], i, n_seq=bp)
        as_ = xattn_heads(xs, W['norm_mem'][i], W['x_w_q'], mem_s[0], mem_s[1], i, n_seq=bs)
        xp, hp = xattn_out(ap, W['x_w_o'], i, xp, W['norm_ffn'][i], BF16)
        xs, hs = xattn_out(as_, W['x_w_o'], i, xs, W['norm_ffn'][i], F32)
        cw = W['ffn_w_dw']
        yp, tails, pre_g, pre_v = matmul_conv(hp, W['ffn_w_up'], cw, i, hs, n_seq=bp, n_out=ffn_dim, gated=True,
                                              out_dtype=BF16)
        outs_p['ffn'].append(tails[:, SUBLANES - (cw.shape[1] - 1):])
        buf = states['ffn_conv'][i]
        ys = state_conv(pre_g, pre_v, buf, cw[i])
        outs_s['ffn'].append(_next_state(buf, pre_g, bs, ffn_dim))
        xp, xs = matmul(yp, W['ffn_w_down'], i, res=xp, x2=ys, res2=xs)
    finish = lambda x, n_seq, t, outs: (
        rmsnorm(x, W['norm_final'], F32).reshape(n_seq, t, d),
        *(jnp.stack(outs[name]) for name in ('conf', 'k', 'v', 'lf', 's', 'gbuf', 'ffn')))
    return finish(xp, bp, tp, outs_p), finish(xs, bs, ts, outs_s)


def kernel(x_prompt, x_sample, cache_fox_k, cache_fox_v, cache_fox_logf, cache_mem_k, cache_mem_v, state_conf,
           state_gdn, state_gdn_conv, state_ffn_conv, page_table, mem_prompt, norm_mix, norm_mem, norm_ffn,
           norm_final, conf_w_pw1, conf_w_dw, conf_ln_g, conf_ln_b, conf_w_pw2, fox_w_in, fox_b_f, fox_w_o,
           gdn_w_in, gdn_w_conv, gdn_a_log, gdn_dt_bias, gdn_norm_g, gdn_w_o, x_w_q, x_w_kv, x_w_o, ffn_w_up,
           ffn_w_dw, ffn_w_down):
    W = dict(norm_mix=norm_mix, norm_mem=norm_mem, norm_ffn=norm_ffn, norm_final=norm_final,
             conf_w_pw1=conf_w_pw1, conf_w_dw=conf_w_dw, conf_ln_g=conf_ln_g, conf_ln_b=conf_ln_b,
             conf_w_pw2=conf_w_pw2, fox_w_in=fox_w_in, fox_b_f=fox_b_f, fox_w_o=fox_w_o,
             gdn_w_in=gdn_w_in, gdn_w_conv=gdn_w_conv, gdn_a_log=gdn_a_log, gdn_dt_bias=gdn_dt_bias,
             gdn_norm_g=gdn_norm_g, gdn_w_o=gdn_w_o, x_w_q=x_w_q, x_w_o=x_w_o,
             ffn_w_up=ffn_w_up, ffn_w_dw=ffn_w_dw, ffn_w_down=ffn_w_down)
    b, t, d = x_prompt.shape
    bs, ts, _ = x_sample.shape
    depth = x_w_kv.shape[0]
    mem = mem_prompt.shape[1]
    xw = x_w_kv.shape[2] // 2
    xh = xw // HEAD

    mem_rows = mem_prompt.reshape(b * mem, d)
    kv = jnp.stack([matmul(mem_rows, x_w_kv, i) for i in range(depth)])
    mem_k_p = kv[..., :xw].reshape(depth, b, mem, xw)
    mem_v_p = kv[..., xw:].reshape(depth, b, mem, xw)

    states = dict(conf=state_conf, gdn=state_gdn, gdn_conv=state_gdn_conv, ffn_conv=state_ffn_conv,
                  fox=(cache_fox_k, cache_fox_v, cache_fox_logf, page_table))
    mem_s = (cache_mem_k.reshape(depth, bs, mem, xw), cache_mem_v.reshape(depth, bs, mem, xw))
    ((y_p, conf_p, fox_k_p, fox_v_p, fox_lf_p, gdn_p, gdn_conv_p, ffn_conv_p),
     (y_s, conf_s, fox_k_s, fox_v_s, fox_lf_s, gdn_s, gdn_conv_s, ffn_conv_s)) = _trunk(
        x_prompt.reshape(b * t, d), b, x_sample.reshape(bs * ts, d), bs, (mem_k_p, mem_v_p), mem_s, states, W)

    return (y_p, y_s,
            conf_p, fox_k_p, fox_v_p, fox_lf_p, gdn_p, gdn_conv_p,
            mem_k_p.reshape(depth, b, mem, xh, HEAD), mem_v_p.reshape(depth, b, mem, xh, HEAD), ffn_conv_p,
            conf_s, fox_k_s, fox_v_s, fox_lf_s, gdn_s, gdn_conv_s, ffn_conv_s)
```

```python
import functools

import jax
import jax.numpy as jnp
from jax import lax
from jax.experimental import pallas as pl
from jax.experimental.pallas import tpu as pltpu

EPS = 1e-6
HEAD = 128
GDN_CHUNK = 64
SUBLANES = 8
LANES = 128
VMEM_CAP = 58 * 1024 * 1024
BF16 = jnp.bfloat16
F32 = jnp.float32


def _params(semantics, vmem_bytes):
    limit = int(min(VMEM_CAP, max(vmem_bytes * 5 // 4 + (4 << 20), 16 << 20)))
    return pltpu.CompilerParams(dimension_semantics=semantics, vmem_limit_bytes=limit)


def _nbytes(shape, dtype):
    n = 1
    for s in shape:
        n *= s
    return n * jnp.dtype(dtype).itemsize


def _silu(x):
    return x * jax.nn.sigmoid(x)


def _dot(a, b):
    return jnp.dot(a, b, preferred_element_type=F32)


def _dot_nt(a, b):
    return lax.dot_general(a, b, (((1,), (1,)), ((), ())), preferred_element_type=F32)


def _dot_tn(a, b):
    return lax.dot_general(a, b, (((0,), (0,)), ((), ())), preferred_element_type=F32)


def _rmsnorm_rows(x, g):
    return x * lax.rsqrt(jnp.mean(x * x, axis=-1, keepdims=True) + EPS) * g


def _rmsnorm_kernel(x_ref, g_ref, o_ref):
    o_ref[...] = _rmsnorm_rows(x_ref[...], g_ref[...]).astype(o_ref.dtype)


def rmsnorm(x, g, out_dtype):
    m, d = x.shape
    tm = min(m, 256)
    vmem = 2 * (_nbytes((tm, d), F32) + _nbytes((tm, d), out_dtype))
    return pl.pallas_call(
        _rmsnorm_kernel,
        grid=(m // tm,),
        in_specs=[pl.BlockSpec((tm, d), lambda i: (i, 0)),
                  pl.BlockSpec((1, d), lambda i: (0, 0))],
        out_specs=pl.BlockSpec((tm, d), lambda i: (i, 0)),
        out_shape=jax.ShapeDtypeStruct((m, d), out_dtype),
        compiler_params=_params(("parallel",), vmem),
        name="rmsnorm",
    )(x, g.reshape(1, d))


def _row_tile(m, x_dtype, k):
    cap = 1024 if jnp.dtype(x_dtype).itemsize == 2 else 512
    if k > 4096:
        cap //= 4
    tm = min(m, cap)
    assert m % tm == 0
    return tm


def _col_tile(n, cap):
    tn = min(n, cap)
    assert n % tn == 0
    return tn


def _mm_kernel(*refs, has_res, has_x2):
    refs = list(refs)
    x_ref, w_ref = refs[:2]
    del refs[:2]
    res_ref = refs.pop(0) if has_res else None
    x2_ref = refs.pop(0) if has_x2 else None
    res2_ref = refs.pop(0) if has_x2 and has_res else None
    o_ref = refs.pop(0)
    o2_ref = refs.pop(0) if has_x2 else None
    wb_ref, = refs

    @pl.when(pl.program_id(1) == 0)
    def _():
        wb_ref[...] = w_ref[...].astype(BF16)
        if has_x2:
            acc2 = _dot(x2_ref[...].astype(BF16), wb_ref[...])
            if has_res:
                acc2 = res2_ref[...] + acc2
            o2_ref[...] = acc2

    acc = _dot(x_ref[...].astype(BF16), wb_ref[...])
    if has_res:
        acc = res_ref[...] + acc
    o_ref[...] = acc.astype(o_ref.dtype)


def _stacked(w, layer):
    return (w[None], 0) if w.ndim == 2 else (w, layer)


def _x2_spec(x2):
    return pl.BlockSpec(x2.shape, lambda n, i: (0, 0), pipeline_mode=pl.Buffered(1))


def matmul(x, w, layer=0, *, col_off=0, n_out=None, res=None, out_dtype=F32, x2=None, res2=None):
    w, layer = _stacked(w, layer)
    m, k = x.shape
    n_out = w.shape[2] - col_off if n_out is None else n_out
    deep = k > 4096
    tn = _col_tile(n_out, 512)
    assert col_off % tn == 0
    off = col_off // tn
    tm = _row_tile(m, x.dtype, k)
    w_bufs = 1 if deep else 2
    w_mode = dict(pipeline_mode=pl.Buffered(1)) if deep else {}
    tile = pl.BlockSpec((tm, tn), lambda n, i: (i, n))
    in_specs = [pl.BlockSpec((tm, k), lambda n, i: (i, 0)),
                pl.BlockSpec((None, k, tn), lambda n, i: (layer, 0, n + off), **w_mode)]
    args = [x, w]
    out_specs = [tile]
    out_shape = [jax.ShapeDtypeStruct((m, n_out), out_dtype)]
    vmem = (2 * _nbytes((tm, k), x.dtype) + w_bufs * _nbytes((k, tn), F32) + _nbytes((k, tn), BF16)
            + 2 * _nbytes((tm, tn), out_dtype))
    if res is not None:
        in_specs.append(tile)
        args.append(res)
        vmem += 2 * _nbytes((tm, tn), F32)
    if x2 is not None:
        m2 = x2.shape[0]
        tile2 = pl.BlockSpec((m2, tn), lambda n, i: (0, n))
        in_specs.append(_x2_spec(x2))
        args.append(x2)
        if res is not None:
            in_specs.append(tile2)
            args.append(res2)
        out_specs.append(tile2)
        out_shape.append(jax.ShapeDtypeStruct((m2, n_out), F32))
        vmem += _nbytes(x2.shape, x2.dtype) + 4 * _nbytes((m2, tn), F32)
    out = pl.pallas_call(
        functools.partial(_mm_kernel, has_res=res is not None, has_x2=x2 is not None),
        grid=(n_out // tn, m // tm),
        in_specs=in_specs,
        out_specs=out_specs,
        out_shape=out_shape,
        scratch_shapes=[pltpu.VMEM((k, tn), BF16)],
        compiler_params=_params(("arbitrary", "arbitrary"), vmem),
        name="matmul",
    )(*args)
    return out[0] if x2 is None else out


def _glu_kernel(x_ref, wa_ref, wb_ref, x2_ref, o_ref, o2_ref, wab_ref, wbb_ref):
    glu = lambda x: _dot(x, wab_ref[...]) * jax.nn.sigmoid(_dot(x, wbb_ref[...]))

    @pl.when(pl.program_id(1) == 0)
    def _():
        wab_ref[...] = wa_ref[...].astype(BF16)
        wbb_ref[...] = wb_ref[...].astype(BF16)
        o2_ref[...] = glu(x2_ref[...].astype(BF16))

    o_ref[...] = glu(x_ref[...].astype(BF16))


def matmul_glu(x, w, layer, x2):
    m, k = x.shape
    m2 = x2.shape[0]
    n = w.shape[2] // 2
    tn = _col_tile(n, 256)
    tm = _row_tile(m, x.dtype, k)
    nb = n // tn
    vmem = (2 * _nbytes((tm, k), x.dtype) + 4 * _nbytes((k, tn), F32) + 2 * _nbytes((k, tn), BF16)
            + 2 * _nbytes((tm, tn), F32) + _nbytes(x2.shape, x2.dtype) + 2 * _nbytes((m2, tn), F32))
    return pl.pallas_call(
        _glu_kernel,
        grid=(nb, m // tm),
        in_specs=[pl.BlockSpec((tm, k), lambda j, i: (i, 0)),
                  pl.BlockSpec((None, k, tn), lambda j, i: (layer, 0, j)),
                  pl.BlockSpec((None, k, tn), lambda j, i: (layer, 0, j + nb)),
                  _x2_spec(x2)],
        out_specs=[pl.BlockSpec((tm, tn), lambda j, i: (i, j)), pl.BlockSpec((m2, tn), lambda j, i: (0, j))],
        out_shape=[jax.ShapeDtypeStruct((m, n), F32), jax.ShapeDtypeStruct((m2, n), F32)],
        scratch_shapes=[pltpu.VMEM((k, tn), BF16), pltpu.VMEM((k, tn), BF16)],
        compiler_params=_params(("arbitrary", "arbitrary"), vmem),
        name="matmul_glu",
    )(x, w, w, x2)


def _mm_conv_kernel(*refs, width, tiles_per_seq, gated):
    if gated:
        x_ref, wg_ref, wv_ref, cw_ref, x2_ref, y_ref, tail_ref, g2_ref, v2_ref, wgb_ref, wvb_ref, ext_ref = refs
    else:
        x_ref, wg_ref, cw_ref, x2_ref, y_ref, tail_ref, g2_ref, wgb_ref, ext_ref = refs
    i = pl.program_id(1)

    @pl.when(i == 0)
    def _():
        x2 = x2_ref[...].astype(BF16)
        wgb_ref[...] = wg_ref[...].astype(BF16)
        g2_ref[...] = _dot(x2, wgb_ref[...])
        if gated:
            wvb_ref[...] = wv_ref[...].astype(BF16)
            v2_ref[...] = _dot(x2, wvb_ref[...])

    @pl.when(i % tiles_per_seq == 0)
    def _():
        ext_ref[0:SUBLANES, :] = jnp.zeros((SUBLANES, ext_ref.shape[1]), F32)

    x = x_ref[...].astype(BF16)
    g = _dot(x, wgb_ref[...])
    tm = g.shape[0]
    ext_ref[SUBLANES:SUBLANES + tm, :] = g
    cw = cw_ref[...]
    y = g * cw[width - 1:width, :]
    for j in range(width - 1):
        y = y + ext_ref[pl.ds(SUBLANES - (width - 1) + j, tm), :] * cw[j:j + 1, :]
    y = _silu(y)
    if gated:
        y = y * _dot(x, wvb_ref[...])
    y_ref[...] = y.astype(y_ref.dtype)
    tail = g[tm - SUBLANES:tm, :]
    tail_ref[0] = tail
    ext_ref[0:SUBLANES, :] = tail


def matmul_conv(x, w, cw, layer, x2, *, n_seq, n_out, gated, out_dtype):
    m, k = x.shape
    m2 = x2.shape[0]
    width = cw.shape[1]
    tn = _col_tile(n_out, 256 if gated else 512)
    t = m // n_seq
    tm = _row_tile(t, x.dtype, k)
    tps = t // tm
    nb = n_out // tn
    in_specs = [pl.BlockSpec((tm, k), lambda j, i: (i, 0)),
                pl.BlockSpec((None, k, tn), lambda j, i: (layer, 0, j))]
    args = [x, w]
    scratch = [pltpu.VMEM((k, tn), BF16)]
    if gated:
        in_specs.append(pl.BlockSpec((None, k, tn), lambda j, i: (layer, 0, j + nb)))
        args.append(w)
        scratch.append(pltpu.VMEM((k, tn), BF16))
    in_specs.append(pl.BlockSpec((None, width, tn), lambda j, i: (layer, 0, j)))
    args.append(cw)
    in_specs.append(_x2_spec(x2))
    args.append(x2)
    scratch.append(pltpu.VMEM((tm + SUBLANES, tn), F32))
    nw = 2 if gated else 1
    vmem = (2 * _nbytes((tm, k), x.dtype) + nw * (2 * _nbytes((k, tn), F32) + _nbytes((k, tn), BF16))
            + 2 * _nbytes((tm, tn), out_dtype) + 6 * _nbytes((tm, tn), F32)
            + _nbytes(x2.shape, x2.dtype) + 2 * nw * _nbytes((m2, tn), F32))
    tile2 = pl.BlockSpec((m2, tn), lambda j, i: (0, j))
    plain2 = jax.ShapeDtypeStruct((m2, n_out), F32)
    return pl.pallas_call(
        functools.partial(_mm_conv_kernel, width=width, tiles_per_seq=tps, gated=gated),
        grid=(nb, m // tm),
        in_specs=in_specs,
        out_specs=[pl.BlockSpec((tm, tn), lambda j, i: (i, j)),
                   pl.BlockSpec((1, SUBLANES, tn), lambda j, i: (i // tps, 0, j))] + [tile2] * nw,
        out_shape=[jax.ShapeDtypeStruct((m, n_out), out_dtype),
                   jax.ShapeDtypeStruct((n_seq, SUBLANES, n_out), F32)] + [plain2] * nw,
        scratch_shapes=scratch,
        compiler_params=_params(("arbitrary", "arbitrary"), vmem),
        name="matmul_conv",
    )(*args)


def _state_conv_kernel(*refs, width, gated):
    if gated:
        u_ref, v_ref, st_ref, cw_ref, y_ref, ext_ref = refs
    else:
        u_ref, st_ref, cw_ref, y_ref, ext_ref = refs
    t = u_ref.shape[0]
    ext_ref[SUBLANES - (width - 1):SUBLANES, :] = st_ref[0]
    ext_ref[SUBLANES:SUBLANES + t, :] = u_ref[...]
    cw = cw_ref[...]
    y = jnp.zeros(u_ref.shape, F32)
    for j in range(width):
        y = y + ext_ref[pl.ds(SUBLANES - (width - 1) + j, t), :] * cw[j:j + 1, :]
    y = _silu(y)
    if gated:
        y = y * v_ref[...]
    y_ref[...] = y


def state_conv(u, v, state, cw):
    n_seq, wm1, n_out = state.shape
    t = u.shape[0] // n_seq
    width = wm1 + 1
    gated = v is not None
    tc = _col_tile(n_out, 256)
    nb = n_out // tc
    in_specs = [pl.BlockSpec((t, tc), lambda s, j: (s, j))]
    args = [u]
    if gated:
        in_specs.append(pl.BlockSpec((t, tc), lambda s, j: (s, j)))
        args.append(v)
    in_specs += [pl.BlockSpec((1, wm1, tc), lambda s, j: (s, 0, j)),
                 pl.BlockSpec((width, tc), lambda s, j: (0, j))]
    args += [state, cw]
    return pl.pallas_call(
        functools.partial(_state_conv_kernel, width=width, gated=gated),
        grid=(n_seq, nb),
        in_specs=in_specs,
        out_specs=pl.BlockSpec((t, tc), lambda s, j: (s, j)),
        out_shape=jax.ShapeDtypeStruct((n_seq * t, n_out), F32),
        scratch_shapes=[pltpu.VMEM((SUBLANES + t, tc), F32)],
        compiler_params=_params(("parallel", "parallel"), 1 << 20),
        name="state_conv",
    )(*args)


def _next_state(state, u, n_seq, n_out):
    wm1 = state.shape[1]
    ext = jnp.concatenate([state, u[:, :n_out].reshape(n_seq, -1, n_out)], axis=1)
    return ext[:, ext.shape[1] - wm1:]


CONF_HALO = 32
CONF_ROWS = 64
CONF_COLS = 256


def _conf_kernel(u_ref, halo_ref, w_ref, lg_ref, lb_ref, o_ref, ext_ref, y_ref, sh_ref, *,
                 taps, tiles_per_seq, halo_is_prev_rows):
    tr, d = u_ref.shape
    if halo_is_prev_rows:
        first = pl.program_id(0) % tiles_per_seq == 0

        @pl.when(first)
        def _():
            ext_ref[0:CONF_HALO, :] = jnp.zeros((CONF_HALO, d), F32)

        @pl.when(jnp.logical_not(first))
        def _():
            ext_ref[0:CONF_HALO, :] = halo_ref[...]
    else:
        ext_ref[0:CONF_HALO, :] = halo_ref[...]
    ext_ref[CONF_HALO:CONF_HALO + tr, :] = u_ref[...]

    rows = min(tr, CONF_ROWS)
    lead = CONF_HALO - (taps - 1)

    def col_block(c, carry):
        c0 = pl.multiple_of(c * CONF_COLS, CONF_COLS)
        w = w_ref[:, pl.ds(c0, CONF_COLS)]
        for r in range(min(SUBLANES, taps)):
            span = tr + (len(range(r, taps, SUBLANES)) - 1) * SUBLANES
            sh_ref[r, 0:span, :] = ext_ref[pl.ds(lead + r, span), pl.ds(c0, CONF_COLS)]
        for r0 in range(0, tr, rows):
            acc = jnp.zeros((rows, CONF_COLS), F32)
            for j in range(taps):
                r = j % SUBLANES
                acc = acc + sh_ref[r, r0 + j - r:r0 + j - r + rows, :] * w[j:j + 1, :]
            y_ref[r0:r0 + rows, pl.ds(c0, CONF_COLS)] = acc
        return carry

    lax.fori_loop(0, d // CONF_COLS, col_block, 0)

    def row_block(i, carry):
        r0 = pl.multiple_of(i * rows, rows)
        y = y_ref[pl.ds(r0, rows), :]
        yc = y - jnp.mean(y, axis=-1, keepdims=True)
        yn = yc * lax.rsqrt(jnp.mean(yc * yc, axis=-1, keepdims=True) + EPS)
        yn = yn * lg_ref[...] + lb_ref[...]
        o_ref[pl.ds(r0, rows), :] = _silu(yn).astype(o_ref.dtype)
        return carry

    lax.fori_loop(0, tr // rows, row_block, 0)


def conformer_conv(u, halo, w_dw, ln_g, ln_b, *, n_seq, out_dtype):
    m, d = u.shape
    taps = w_dw.shape[0]
    assert taps - 1 <= CONF_HALO and d % CONF_COLS == 0
    t = m // n_seq
    tr = min(t, 256)
    assert t % tr == 0 and tr % SUBLANES == 0
    tps = t // tr
    prev = halo is None
    if prev:
        assert tr % CONF_HALO == 0
        per = tr // CONF_HALO
        halo_arr = u
        halo_spec = pl.BlockSpec((CONF_HALO, d), lambda i: (jnp.maximum(i * per - 1, 0), 0))
    else:
        assert tps == 1
        halo_arr = halo
        halo_spec = pl.BlockSpec((CONF_HALO, d), lambda i: (i, 0))
    vmem = (2 * _nbytes((tr, d), F32) + 2 * _nbytes((CONF_HALO, d), F32) + 2 * _nbytes((tr, d), out_dtype)
            + _nbytes((2 * tr + CONF_HALO, d), F32) + 4 * _nbytes((32, d), F32))
    return pl.pallas_call(
        functools.partial(_conf_kernel, taps=taps, tiles_per_seq=tps, halo_is_prev_rows=prev),
        grid=(m // tr,),
        in_specs=[pl.BlockSpec((tr, d), lambda i: (i, 0)),
                  halo_spec,
                  pl.BlockSpec((taps, d), lambda i: (0, 0)),
                  pl.BlockSpec((1, d), lambda i: (0, 0)),
                  pl.BlockSpec((1, d), lambda i: (0, 0))],
        out_specs=pl.BlockSpec((tr, d), lambda i: (i, 0)),
        out_shape=jax.ShapeDtypeStruct((m, d), out_dtype),
        scratch_shapes=[pltpu.VMEM((CONF_HALO + tr, d), F32), pltpu.VMEM((tr, d), F32),
                        pltpu.VMEM((SUBLANES, tr + CONF_HALO, CONF_COLS), F32)],
        compiler_params=_params(("arbitrary",), vmem),
        name="conformer_conv",
    )(u, halo_arr, w_dw, ln_g.reshape(1, d), ln_b.reshape(1, d))


def _lane_cumsum(x, seg=None):
    n = x.shape[-1]
    lane = lax.broadcasted_iota(jnp.int32, x.shape, x.ndim - 1)
    pos = lane if seg is None else lane % seg
    d = 1
    while d < (n if seg is None else seg):
        x = x + jnp.where(pos >= d, pltpu.roll(x, d, x.ndim - 1), 0.0)
        d *= 2
    return x


def _logf_kernel(gt_ref, bf_ref, lf_ref, c_ref):
    lf = jax.nn.log_sigmoid(gt_ref[0] + bf_ref[...])
    lf_ref[0] = lf
    c_ref[0] = _lane_cumsum(lf)


def fox_logf(gates_t, b_f):
    n_seq, h, t = gates_t.shape
    spec = pl.BlockSpec((1, h, t), lambda s: (s, 0, 0))
    shape = jax.ShapeDtypeStruct((n_seq, h, t), F32)
    return pl.pallas_call(
        _logf_kernel,
        grid=(n_seq,),
        in_specs=[spec, pl.BlockSpec((h, 1), lambda s: (0, 0))],
        out_specs=[spec, spec],
        out_shape=[shape, shape],
        compiler_params=_params(("parallel",), 1 << 20),
        name="fox_logf",
    )(gates_t, b_f.reshape(h, 1))


FLASH_HEADS = 2


def _flash_logits(q, k_ref, ck_ref, cq, hh, j, *, tk, scale):
    ks = pl.multiple_of(j * tk, tk)
    k = k_ref[pl.ds(ks, tk), hh * HEAD:(hh + 1) * HEAD].astype(BF16)
    return _dot_nt(k, q) * scale + (cq - ck_ref[0, hh, pl.ds(ks, tk), :])


def _flash_update(s, v_ref, hh, carry, j, *, tk, q0, masked):
    m, l, acc = carry
    ks = pl.multiple_of(j * tk, tk)
    v = v_ref[pl.ds(ks, tk), hh * HEAD:(hh + 1) * HEAD].astype(BF16)
    if masked:
        kpos = ks + lax.broadcasted_iota(jnp.int32, s.shape, 0)
        qpos = q0 + lax.broadcasted_iota(jnp.int32, s.shape, 1)
        s = jnp.where(kpos <= qpos, s, -jnp.inf)
    m_new = jnp.maximum(m, jnp.max(s, axis=0, keepdims=True))
    alpha = jnp.exp(m - m_new)
    p = jnp.exp(s - m_new)
    l = alpha * l + jnp.sum(p, axis=0, keepdims=True)
    acc = alpha * acc + _dot_tn(v, p.astype(BF16))
    return m_new, l, acc


def _flash_kernel(q_ref, k_ref, v_ref, cq_ref, ck_ref, o_ref, *, scale):
    tq = q_ref.shape[0]
    nh = q_ref.shape[1] // HEAD
    qi = pl.program_id(2)
    logits = [functools.partial(_flash_logits, q_ref[:, hh * HEAD:(hh + 1) * HEAD].astype(BF16), k_ref, ck_ref,
                                cq_ref[0, hh], hh, tk=tq, scale=scale) for hh in range(nh)]
    update = functools.partial(_flash_update, v_ref=v_ref, tk=tq, q0=qi * tq)
    init = (jnp.full((1, tq), -jnp.inf, F32), jnp.zeros((1, tq), F32), jnp.zeros((HEAD, tq), F32))

    def body(j, carries):
        return tuple(update(logits[hh](j), hh=hh, carry=c, j=j, masked=False) for hh, c in enumerate(carries))

    carries = lax.fori_loop(0, qi, body, (init,) * nh)
    for hh, c in enumerate(carries):
        m, l, acc = update(logits[hh](qi), hh=hh, carry=c, j=qi, masked=True)
        o_ref[:, hh * HEAD:(hh + 1) * HEAD] = (acc / l).T.astype(o_ref.dtype)


def fox_prompt_attention(q, k, v, c, *, n_seq):
    m, width = q.shape
    h = width // HEAD
    t = m // n_seq
    tq = min(t, 512)
    nq = t // tq
    cq = c.reshape(n_seq, h, 1, t)
    ck = c.reshape(n_seq, h, t, 1)
    nh = FLASH_HEADS if h % FLASH_HEADS == 0 else 1
    wblk = nh * HEAD
    vmem = (4 * _nbytes((tq, wblk), F32) + 4 * _nbytes((t, wblk), F32) + 2 * nh * _nbytes((t, LANES), F32)
            + 8 * nh * _nbytes((tq, tq), F32))
    return pl.pallas_call(
        functools.partial(_flash_kernel, scale=HEAD ** -0.5),
        grid=(n_seq, h // nh, nq),
        in_specs=[pl.BlockSpec((tq, wblk), lambda b, hh, i: (b * nq + i, hh)),
                  pl.BlockSpec((t, wblk), lambda b, hh, i: (b, hh)),
                  pl.BlockSpec((t, wblk), lambda b, hh, i: (b, hh)),
                  pl.BlockSpec((1, nh, 1, tq), lambda b, hh, i: (b, hh, 0, i)),
                  pl.BlockSpec((1, nh, t, 1), lambda b, hh, i: (b, hh, 0, 0))],
        out_specs=pl.BlockSpec((tq, wblk), lambda b, hh, i: (b * nq + i, hh)),
        out_shape=jax.ShapeDtypeStruct((m, width), BF16),
        compiler_params=_params(("parallel", "parallel", "arbitrary"), vmem),
        name="fox_prompt_attention",
    )(q, k, v, cq, ck)


def _split3(x):
    hi = x.astype(BF16)
    r = x - hi.astype(F32)
    mid = r.astype(BF16)
    lo = (r - mid.astype(F32)).astype(BF16)
    return hi, mid, lo


def _decode_attend(kget, vget, past_t, causal, qs_ref, cn_ref, s_ref, rep_ref, pv_ref, m_ref, l_ref, acc_ref, *,
                   heads, t, scale):
    keys = past_t.shape[1]
    for hh in range(heads):
        rows = slice(hh * t, (hh + 1) * t)
        s_ref[rows, :] = _dot_nt(qs_ref[rows, :].astype(BF16), kget(hh))
        rep_ref[rows, :] = jnp.broadcast_to(past_t[hh:hh + 1, :], (t, keys))
    s = s_ref[...] * scale + (cn_ref[0] - rep_ref[...])
    if causal:
        qpos = lax.broadcasted_iota(jnp.int32, s.shape, 0) % t
        kpos = lax.broadcasted_iota(jnp.int32, s.shape, 1)
        s = jnp.where(kpos <= qpos, s, -jnp.inf)
    m = m_ref[...]
    m_new = jnp.maximum(m, jnp.max(s, axis=-1, keepdims=True))
    alpha = jnp.exp(m - m_new)
    p = jnp.exp(s - m_new)
    l_ref[...] = alpha * l_ref[...] + jnp.sum(p, axis=-1, keepdims=True)
    m_ref[...] = m_new
    s_ref[...] = p
    for hh in range(heads):
        rows = slice(hh * t, (hh + 1) * t)
        pv_ref[rows, :] = _dot(s_ref[rows, :].astype(BF16), vget(hh))
    acc_ref[...] = alpha * acc_ref[...] + pv_ref[...]


def _decode_kernel(pt_ref, q_ref, kp_ref, vp_ref, lf_ref, kn_ref, vn_ref, cn_ref, cnrow_ref, o_ref,
                   qs_ref, s_ref, rep_ref, pv_ref, m_ref, l_ref, acc_ref, carry_ref, *, heads, t, scale):
    del pt_ref
    p = pl.program_id(1)

    @pl.when(p == 0)
    def _():
        m_ref[...] = jnp.full(m_ref.shape, -jnp.inf, F32)
        l_ref[...] = jnp.zeros(l_ref.shape, F32)
        acc_ref[...] = jnp.zeros(acc_ref.shape, F32)
        carry_ref[...] = jnp.zeros(carry_ref.shape, F32)
        for hh in range(heads):
            qs_ref[hh * t:(hh + 1) * t, :] = q_ref[:, hh * HEAD:(hh + 1) * HEAD]

    attend = functools.partial(_decode_attend, qs_ref=qs_ref, cn_ref=cn_ref, s_ref=s_ref, rep_ref=rep_ref,
                               pv_ref=pv_ref, m_ref=m_ref, l_ref=l_ref, acc_ref=acc_ref,
                               heads=heads, t=t, scale=scale)

    x = lf_ref[...]
    n = x.shape[0]
    row = lax.broadcasted_iota(jnp.int32, (n, n), 0)
    col = lax.broadcasted_iota(jnp.int32, (n, n), 1)
    after = jnp.where(row > col, 1.0, 0.0).astype(BF16)
    ones = jnp.ones((n, n), BF16)
    hi, mid, lo = _split3(x)
    later_t = (_dot_tn(hi, after) + _dot_tn(mid, after)) + _dot_tn(lo, after)
    past_t = -(later_t + carry_ref[...])
    carry_ref[...] = carry_ref[...] + ((_dot_tn(hi, ones) + _dot_tn(mid, ones)) + _dot_tn(lo, ones))
    kt = jnp.swapaxes(kp_ref[...].astype(BF16), 0, 1)
    vt = jnp.swapaxes(vp_ref[...].astype(BF16), 0, 1)
    attend(lambda hh: kt[hh], lambda hh: vt[hh], past_t, False)

    @pl.when(p == pl.num_programs(1) - 1)
    def _():
        attend(lambda hh: kn_ref[:, hh * HEAD:(hh + 1) * HEAD].astype(BF16),
               lambda hh: vn_ref[:, hh * HEAD:(hh + 1) * HEAD].astype(BF16), cnrow_ref[...], True)
        for hh in range(heads):
            rows = slice(hh * t, (hh + 1) * t)
            o_ref[:, hh * HEAD:(hh + 1) * HEAD] = acc_ref[rows, :] / l_ref[rows, :]


def fox_sample_attention(q, k_new, v_new, c_new, cache_k, cache_v, cache_logf, slot, page_table):
    n_seq, n_pages = page_table.shape
    m, width = q.shape
    h = width // HEAD
    t = m // n_seq
    page = cache_k.shape[2]
    pad = lambda a: jnp.pad(a.reshape(n_seq, t, width), ((0, 0), (0, page - t), (0, 0)))
    cn_col = c_new[:, :, :t].reshape(n_seq, h * t, 1)
    cn_row = jnp.pad(c_new[:, :, :t], ((0, 0), (0, 0), (0, page - t)))
    last = n_pages - 1
    grid_spec = pltpu.PrefetchScalarGridSpec(
        num_scalar_prefetch=1,
        grid=(n_seq, n_pages),
        in_specs=[pl.BlockSpec((t, width), lambda b, p, pt: (b, 0)),
                  pl.BlockSpec((None, None, page, h, HEAD), lambda b, p, pt: (slot, pt[b, last - p], 0, 0, 0)),
                  pl.BlockSpec((None, None, page, h, HEAD), lambda b, p, pt: (slot, pt[b, last - p], 0, 0, 0)),
                  pl.BlockSpec((None, None, page, h), lambda b, p, pt: (slot, pt[b, last - p], 0, 0)),
                  pl.BlockSpec((None, page, width), lambda b, p, pt: (b, 0, 0)),
                  pl.BlockSpec((None, page, width), lambda b, p, pt: (b, 0, 0)),
                  pl.BlockSpec((1, h * t, 1), lambda b, p, pt: (b, 0, 0)),
                  pl.BlockSpec((None, h, page), lambda b, p, pt: (b, 0, 0))],
        out_specs=pl.BlockSpec((t, width), lambda b, p, pt: (b, 0)),
        scratch_shapes=[pltpu.VMEM((h * t, HEAD), F32), pltpu.VMEM((h * t, page), F32),
                        pltpu.VMEM((h * t, page), F32), pltpu.VMEM((h * t, HEAD), F32),
                        pltpu.VMEM((h * t, 1), F32), pltpu.VMEM((h * t, 1), F32),
                        pltpu.VMEM((h * t, HEAD), F32), pltpu.VMEM((h, page), F32)],
    )
    vmem = 8 * _nbytes((page, width), F32) + 2 * _nbytes((page, width), BF16)
    return pl.pallas_call(
        functools.partial(_decode_kernel, heads=h, t=t, scale=HEAD ** -0.5),
        grid_spec=grid_spec,
        out_shape=jax.ShapeDtypeStruct((m, width), F32),
        compiler_params=_params(("arbitrary", "arbitrary"), vmem),
        name="fox_sample_attention",
    )(page_table, q, cache_k, cache_v, cache_logf, pad(k_new), pad(v_new), cn_col, cn_row)


def _gdn_gate_kernel(gt_ref, alog_ref, dtb_ref, beta_ref, gc_ref, *, heads, t_valid):
    g_all = gt_ref[0]
    lane = lax.broadcasted_iota(jnp.int32, (heads, g_all.shape[1]), 1)
    valid = lane < t_valid
    beta = jnp.where(valid, jax.nn.sigmoid(g_all[:heads]), 0.0)
    g = -jnp.exp(alog_ref[...]) * jax.nn.softplus(g_all[heads:] + dtb_ref[...])
    g = jnp.where(valid, g, 0.0)
    beta_ref[0] = beta
    gc_ref[0] = _lane_cumsum(g, GDN_CHUNK)


def gdn_gates(gates_t, a_log, dt_bias, *, t_valid):
    n_seq, h2, t = gates_t.shape
    h = h2 // 2
    out_spec = pl.BlockSpec((1, h, t), lambda s: (s, 0, 0))
    shape = jax.ShapeDtypeStruct((n_seq, h, t), F32)
    return pl.pallas_call(
        functools.partial(_gdn_gate_kernel, heads=h, t_valid=t_valid),
        grid=(n_seq,),
        in_specs=[pl.BlockSpec((1, h2, t), lambda s: (s, 0, 0)),
                  pl.BlockSpec((h, 1), lambda s: (0, 0)),
                  pl.BlockSpec((h, 1), lambda s: (0, 0))],
        out_specs=[out_spec, out_spec],
        out_shape=[shape, shape],
        compiler_params=_params(("parallel",), 1 << 20),
        name="gdn_gates",
    )(gates_t, a_log.reshape(h, 1), dt_bias.reshape(h, 1))


def _row_to_col(row, eye):
    return jnp.sum(jnp.where(eye, row, 0.0), axis=1, keepdims=True)


def _gdn_kernel(q_ref, k_ref, v_ref, z_ref, beta_ref, gc_ref, s0_ref, ng_ref, o_ref, s_ref,
                u_s, wq_s, kd_s, in_s, *, hb, n_chunks):
    c_len = GDN_CHUNK
    n = hb * c_len
    row = lax.broadcasted_iota(jnp.int32, (n, n), 0)
    col = lax.broadcasted_iota(jnp.int32, (n, n), 1)
    eye = row == col
    same_head = (row // c_len) == (col // c_len)
    lower = jnp.logical_and(same_head, col <= row)
    strict = jnp.logical_and(same_head, col < row)
    ng = ng_ref[...]
    heads = range(hb)
    stack = lambda f: jnp.concatenate([f(hh) for hh in heads], axis=0)
    blk = lambda a, hh: a[hh * c_len:(hh + 1) * c_len, hh * HEAD:(hh + 1) * HEAD]

    def last_of(gc_row, hh):
        return gc_row[:, (hh + 1) * c_len - 1:(hh + 1) * c_len]

    def prepare(c, carry):
        r0 = pl.multiple_of(c * c_len, c_len)

        def unit(ref, hh, scale):
            a = ref[pl.ds(r0, c_len), hh * HEAD:(hh + 1) * HEAD]
            return a * lax.rsqrt(jnp.sum(a * a, axis=-1, keepdims=True) + EPS) * scale

        qn = stack(lambda hh: unit(q_ref, hh, HEAD ** -0.5))
        kn = stack(lambda hh: unit(k_ref, hh, 1.0))
        v = stack(lambda hh: v_ref[pl.ds(r0, c_len), hh * HEAD:(hh + 1) * HEAD])
        beta_row = beta_ref[0, 0, pl.ds(c, 1), :]
        gc_row = gc_ref[0, 0, pl.ds(c, 1), :]
        beta_col = _row_to_col(beta_row, eye)
        gc_col = _row_to_col(gc_row, eye)
        gc_last_col = stack(lambda hh: jnp.broadcast_to(last_of(gc_row, hh), (c_len, 1)))
        decay = jnp.exp(jnp.where(lower, gc_col - gc_row, -jnp.inf))
        kn16 = kn.astype(BF16)
        kk_qk = _dot_nt(jnp.concatenate([kn16, qn.astype(BF16)], axis=0), kn16)
        lmat = jnp.where(strict, kk_qk[:n] * decay * beta_col, 0.0)
        inv = jnp.where(eye, 1.0, 0.0) - lmat
        p16 = lmat.astype(BF16)
        p16 = _dot(p16, p16).astype(BF16)
        span = 4
        while span < c_len:
            both = _dot(jnp.concatenate([p16, inv.astype(BF16)], axis=0), p16)
            p16 = both[:n].astype(BF16)
            inv = inv + both[n:]
            span *= 2
        inv = inv + _dot(inv.astype(BF16), p16)
        e_col = jnp.exp(gc_col)
        rhs = jnp.concatenate([v * beta_col, kn * (beta_col * e_col)], axis=1)
        sol = _dot(inv.astype(BF16), rhs.astype(BF16))
        u_s[c] = sol[:, :HEAD]
        wq_s[c, 0:n, :] = sol[:, HEAD:].astype(BF16)
        wq_s[c, n:2 * n, :] = (qn * e_col).astype(BF16)
        in_s[c] = (kk_qk[n:] * decay).astype(BF16)
        kd_s[c] = (kn * jnp.exp(gc_last_col - gc_col)).astype(BF16)
        return carry

    lax.fori_loop(0, n_chunks, prepare, 0, unroll=2 if n_chunks % 2 == 0 else 1)

    head_of_row = lax.broadcasted_iota(jnp.int32, (n, HEAD), 0) // c_len

    def advance(c, s_wide):
        r0 = pl.multiple_of(c * c_len, c_len)
        r = _dot(wq_s[c], s_wide.astype(BF16))
        w_s = stack(lambda hh: blk(r, hh))
        q_s = stack(lambda hh: blk(r[n:], hh))
        v_new = u_s[c] - w_s
        o = q_s + _dot(in_s[c], v_new.astype(BF16))
        v_wide = jnp.concatenate([jnp.where(head_of_row == hh, v_new, 0.0) for hh in heads], axis=1).astype(BF16)
        gc_row = gc_ref[0, 0, pl.ds(c, 1), :]
        g_last = jnp.concatenate([jnp.broadcast_to(jnp.exp(last_of(gc_row, hh)), (1, HEAD)) for hh in heads], axis=1)
        s_next = s_wide * g_last + _dot_tn(kd_s[c], v_wide)
        for hh in heads:
            cols = slice(hh * HEAD, (hh + 1) * HEAD)
            oh = o[hh * c_len:(hh + 1) * c_len]
            on = oh * lax.rsqrt(jnp.mean(oh * oh, axis=-1, keepdims=True) + EPS) * ng
            z = z_ref[pl.ds(r0, c_len), cols]
            o_ref[pl.ds(r0, c_len), cols] = (on * _silu(z)).astype(o_ref.dtype)
        return s_next

    s_wide = lax.fori_loop(0, n_chunks, advance, jnp.concatenate([s0_ref[0, hh] for hh in heads], axis=1))
    for hh in heads:
        s_ref[0, hh] = s_wide[:, hh * HEAD:(hh + 1) * HEAD]


def gdn_mix(qkv, z, z_col_off, beta, gc, s0, norm_g, *, n_seq, out_dtype):
    m = qkv.shape[0]
    h = qkv.shape[1] // (3 * HEAD)
    t = m // n_seq
    n_chunks = t // GDN_CHUNK
    hb = 4
    assert h % hb == 0
    nhb = h // hb
    wblk = hb * HEAD
    zoff = z_col_off // wblk
    n = hb * GDN_CHUNK
    by_block = lambda a: jnp.transpose(a.reshape(n_seq, nhb, hb, n_chunks, GDN_CHUNK), (0, 1, 3, 2, 4)).reshape(
        n_seq, nhb, n_chunks, n)
    beta, gc = by_block(beta), by_block(gc)
    seq_spec = lambda off: pl.BlockSpec((t, wblk), lambda b, j: (b, j + off))
    gate_spec = pl.BlockSpec((1, 1, n_chunks, n), lambda b, j: (b, j, 0, 0))
    state_spec = pl.BlockSpec((1, hb, HEAD, HEAD), lambda b, j: (b, j, 0, 0))
    scratch = [pltpu.VMEM((n_chunks, n, HEAD), F32), pltpu.VMEM((n_chunks, 2 * n, HEAD), BF16),
               pltpu.VMEM((n_chunks, n, HEAD), BF16), pltpu.VMEM((n_chunks, n, n), BF16)]
    vmem = (2 * 4 * _nbytes((t, wblk), F32) + 2 * _nbytes((t, wblk), out_dtype)
            + _nbytes((n_chunks, n, HEAD), F32) + _nbytes((n_chunks, 3 * n + 2 * n, HEAD), BF16))
    return pl.pallas_call(
        functools.partial(_gdn_kernel, hb=hb, n_chunks=n_chunks),
        grid=(n_seq, nhb),
        in_specs=[seq_spec(0), seq_spec(nhb), seq_spec(2 * nhb), seq_spec(zoff),
                  gate_spec, gate_spec, state_spec,
                  pl.BlockSpec((1, HEAD), lambda b, j: (0, 0))],
        out_specs=[pl.BlockSpec((t, wblk), lambda b, j: (b, j)), state_spec],
        out_shape=[jax.ShapeDtypeStruct((m, h * HEAD), out_dtype),
                   jax.ShapeDtypeStruct((n_seq, h, HEAD, HEAD), F32)],
        scratch_shapes=scratch,
        compiler_params=_params(("parallel", "parallel"), vmem),
        name="gdn_mix",
    )(qkv, qkv, qkv, z, beta, gc, s0, norm_g.reshape(1, HEAD))


def _xattn_q_kernel(x_ref, g_ref, w_ref, mk_ref, mv_ref, o_ref, wb_ref, *, heads, scale):
    @pl.when(pl.program_id(0) == 0)
    def _():
        wb_ref[...] = w_ref[...].astype(BF16)

    q = _dot(_rmsnorm_rows(x_ref[...], g_ref[...]).astype(BF16), wb_ref[...])
    for hh in range(heads):
        cols = slice(hh * HEAD, (hh + 1) * HEAD)
        s = _dot_nt(q[:, cols].astype(BF16), mk_ref[0, :, cols].astype(BF16)) * scale
        p = jnp.exp(s - jnp.max(s, axis=-1, keepdims=True))
        o = _dot(p.astype(BF16), mv_ref[0, :, cols].astype(BF16)) / jnp.sum(p, axis=-1, keepdims=True)
        o_ref[:, cols] = o.astype(o_ref.dtype)


def xattn_heads(x, g, w_q, mem_k, mem_v, layer, *, n_seq):
    m, d = x.shape
    width = w_q.shape[2]
    t = m // n_seq
    tm = min(t, _row_tile(m, x.dtype, d))
    tps = t // tm
    mem = mem_k.shape[2]
    vmem = (3 * _nbytes((tm, d), F32) + 3 * _nbytes((d, width), F32) + 4 * _nbytes((mem, width), F32)
            + 2 * _nbytes((tm, width), F32) + 8 * _nbytes((tm, mem), F32))
    return pl.pallas_call(
        functools.partial(_xattn_q_kernel, heads=width // HEAD, scale=HEAD ** -0.5),
        grid=(m // tm,),
        in_specs=[pl.BlockSpec((tm, d), lambda i: (i, 0)),
                  pl.BlockSpec((1, d), lambda i: (0, 0)),
                  pl.BlockSpec((None, d, width), lambda i: (layer, 0, 0)),
                  pl.BlockSpec((None, 1, mem, width), lambda i: (layer, i // tps, 0, 0)),
                  pl.BlockSpec((None, 1, mem, width), lambda i: (layer, i // tps, 0, 0))],
        out_specs=pl.BlockSpec((tm, width), lambda i: (i, 0)),
        out_shape=jax.ShapeDtypeStruct((m, width), F32),
        scratch_shapes=[pltpu.VMEM((d, width), BF16)],
        compiler_params=_params(("arbitrary",), vmem),
        name="xattn_heads",
    )(x, g.reshape(1, d), w_q, mem_k, mem_v)


def _xattn_o_kernel(a_ref, w_ref, res_ref, g_ref, x_ref, h_ref, wb_ref):
    @pl.when(pl.program_id(0) == 0)
    def _():
        wb_ref[...] = w_ref[...].astype(BF16)

    x = res_ref[...] + _dot(a_ref[...].astype(BF16), wb_ref[...])
    x_ref[...] = x
    h_ref[...] = _rmsnorm_rows(x, g_ref[...]).astype(h_ref.dtype)


def xattn_out(a, w_o, layer, res, g, h_dtype):
    m, width = a.shape
    d = w_o.shape[2]
    tm = min(m, 256)
    vmem = (2 * _nbytes((tm, width), F32) + 3 * _nbytes((width, d), F32) + 4 * _nbytes((tm, d), F32)
            + 2 * _nbytes((tm, d), h_dtype) + 2 * _nbytes((tm, d), F32))
    return pl.pallas_call(
        _xattn_o_kernel,
        grid=(m // tm,),
        in_specs=[pl.BlockSpec((tm, width), lambda i: (i, 0)),
                  pl.BlockSpec((None, width, d), lambda i: (layer, 0, 0)),
                  pl.BlockSpec((tm, d), lambda i: (i, 0)),
                  pl.BlockSpec((1, d), lambda i: (0, 0))],
        out_specs=[pl.BlockSpec((tm, d), lambda i: (i, 0)), pl.BlockSpec((tm, d), lambda i: (i, 0))],
        out_shape=[jax.ShapeDtypeStruct((m, d), F32), jax.ShapeDtypeStruct((m, d), h_dtype)],
        scratch_shapes=[pltpu.VMEM((width, d), BF16)],
        compiler_params=_params(("arbitrary",), vmem),
        name="xattn_out",
    )(a, w_o, res, g.reshape(1, d))


def _pad_lanes(a, mult):
    pad = (-a.shape[-1]) % mult
    return a if pad == 0 else jnp.pad(a, [(0, 0)] * (a.ndim - 1) + [(0, pad)])


def _trunk(xp, bp, xs, bs, mem_p, mem_s, states, W):
    d = xp.shape[1]
    tp, ts = xp.shape[0] // bp, xs.shape[0] // bs
    depth = W['norm_mix'].shape[0]
    ffn_dim = W['ffn_w_dw'].shape[2]
    heads = d // HEAD
    hv = heads * HEAD
    outs_p = dict(conf=[], k=[], v=[], lf=[], s=[], gbuf=[], ffn=[])
    outs_s = dict(conf=[], k=[], v=[], lf=[], s=[], gbuf=[], ffn=[])
    fox_w_in = W['fox_w_in'].astype(BF16)
    gdn_w_in = W['gdn_w_in'].astype(BF16)

    def logf(gate, n_seq, t, slot):
        gates_t = _pad_lanes(jnp.transpose(gate.reshape(n_seq, t, heads), (0, 2, 1)), LANES)
        lf_t, c_t = fox_logf(gates_t, W['fox_b_f'][slot])
        return jnp.transpose(lf_t[:, :, :t], (0, 2, 1)), c_t

    def gdn_group(qkv, z, gate, s0, n_seq, t, t_pad, slot, out_dtype):
        gates_t = jnp.transpose(gate.reshape(n_seq, t, 2 * heads), (0, 2, 1))
        gates_t = jnp.pad(gates_t, ((0, 0), (0, 0), (0, max(t_pad, LANES) - t)))
        beta, gc = gdn_gates(gates_t, W['gdn_a_log'][slot], W['gdn_dt_bias'][slot], t_valid=t)
        return gdn_mix(qkv, z, 0, beta[:, :, :t_pad], gc[:, :, :t_pad], s0, W['gdn_norm_g'][slot],
                       n_seq=n_seq, out_dtype=out_dtype)

    for i in range(depth):
        kind, slot = i % 3, i // 3
        hp = rmsnorm(xp, W['norm_mix'][i], BF16)
        hs = rmsnorm(xs, W['norm_mix'][i], F32)
        if kind == 0:
            up, us = matmul_glu(hp, W['conf_w_pw1'], slot, hs)
            taps = W['conf_w_dw'].shape[1]
            outs_p['conf'].append(up.reshape(bp, tp, d)[:, tp - (taps - 1):])
            buf = states['conf'][slot]
            halo = jnp.pad(buf, ((0, 0), (CONF_HALO - (taps - 1), 0), (0, 0))).reshape(bs * CONF_HALO, d)
            outs_s['conf'].append(_next_state(buf, us, bs, d))
            conv = functools.partial(conformer_conv, w_dw=W['conf_w_dw'][slot], ln_g=W['conf_ln_g'][slot],
                                     ln_b=W['conf_ln_b'][slot])
            yp = conv(up, None, n_seq=bp, out_dtype=BF16)
            ys = conv(us, halo, n_seq=bs, out_dtype=F32)
            xp, xs = matmul(yp, W['conf_w_pw2'], slot, res=xp, x2=ys, res2=xs)
        elif kind == 1:
            w_in = fox_w_in
            qp, qs = matmul(hp, w_in, slot, col_off=0, n_out=hv, x2=hs)
            kp, ks = matmul(hp, w_in, slot, col_off=hv, n_out=hv, x2=hs)
            vp, vs = matmul(hp, w_in, slot, col_off=2 * hv, n_out=hv, x2=hs)
            gate_p, gate_s = matmul(hp, w_in[slot, :, 3 * hv:], x2=hs)
            lf_p, c_p = logf(gate_p, bp, tp, slot)
            lf_s, c_s = logf(gate_s, bs, ts, slot)
            op = fox_prompt_attention(qp, kp, vp, c_p, n_seq=bp)
            cache_k, cache_v, cache_lf, page_table = states['fox']
            os_ = fox_sample_attention(qs, ks, vs, c_s, cache_k, cache_v, cache_lf, slot, page_table)
            xp, xs = matmul(op, W['fox_w_o'], slot, res=xp, x2=os_, res2=xs)
            for outs, k, v, lf, n_seq, t in ((outs_p, kp, vp, lf_p, bp, tp), (outs_s, ks, vs, lf_s, bs, ts)):
                outs['k'].append(k.reshape(n_seq, t, heads, HEAD))
                outs['v'].append(v.reshape(n_seq, t, heads, HEAD))
                outs['lf'].append(lf)
        else:
            w_in = gdn_w_in
            cw = W['gdn_w_conv']
            n_qkv = cw.shape[2]
            qkv_p, tails, pre_s = matmul_conv(hp, w_in, cw, slot, hs, n_seq=bp, n_out=n_qkv, gated=False,
                                              out_dtype=F32)
            outs_p['gbuf'].append(tails[:, SUBLANES - (cw.shape[1] - 1):])
            zp, zs = matmul(hp, w_in, slot, col_off=n_qkv, n_out=hv, x2=hs)
            gate_p, gate_s = matmul(hp, w_in[slot, :, n_qkv + hv:], x2=hs)
            op, s_p = gdn_group(qkv_p, zp, gate_p, jnp.zeros((bp, heads, HEAD, HEAD), F32), bp, tp, tp, slot, BF16)
            buf = states['gdn_conv'][slot]
            qkv_s = state_conv(pre_s, None, buf, cw[slot])
            outs_s['gbuf'].append(_next_state(buf, pre_s, bs, n_qkv))
            ts_pad = -(-ts // GDN_CHUNK) * GDN_CHUNK
            rows = lambda a: jnp.pad(a.reshape(bs, ts, -1), ((0, 0), (0, ts_pad - ts), (0, 0))).reshape(
                bs * ts_pad, -1)
            os_, s_s = gdn_group(rows(qkv_s), rows(zs), gate_s, states['gdn'][slot], bs, ts, ts_pad, slot, F32)
            os_ = os_.reshape(bs, ts_pad, hv)[:, :ts].reshape(bs * ts, hv)
            xp, xs = matmul(op, W['gdn_w_o'], slot, res=xp, x2=os_, res2=xs)
            outs_p['s'].append(s_p)
            outs_s['s'].append(s_s)
        ap = xattn_heads(xp, W['norm_mem'][i], W['x_w_q'], mem_p[0], mem_p[1], i, n_seq=bp)
        as_ = xattn_heads(xs, W['norm_mem'][i], W['x_w_q'], mem_s[0], mem_s[1], i, n_seq=bs)
        xp, hp = xattn_out(ap, W['x_w_o'], i, xp, W['norm_ffn'][i], BF16)
        xs, hs = xattn_out(as_, W['x_w_o'], i, xs, W['norm_ffn'][i], F32)
        cw = W['ffn_w_dw']
        yp, tails, pre_g, pre_v = matmul_conv(hp, W['ffn_w_up'], cw, i, hs, n_seq=bp, n_out=ffn_dim, gated=True,
                                              out_dtype=BF16)
        outs_p['ffn'].append(tails[:, SUBLANES - (cw.shape[1] - 1):])
        buf = states['ffn_conv'][i]
        ys = state_conv(pre_g, pre_v, buf, cw[i])
        outs_s['ffn'].append(_next_state(buf, pre_g, bs, ffn_dim))
        xp, xs = matmul(yp, W['ffn_w_down'], i, res=xp, x2=ys, res2=xs)
    finish = lambda x, n_seq, t, outs: (
        rmsnorm(x, W['norm_final'], F32).reshape(n_seq, t, d),
        *(jnp.stack(outs[name]) for name in ('conf', 'k', 'v', 'lf', 's', 'gbuf', 'ffn')))
    return finish(xp, bp, tp, outs_p), finish(xs, bs, ts, outs_s)


def kernel(x_prompt, x_sample, cache_fox_k, cache_fox_v, cache_fox_logf, cache_mem_k, cache_mem_v, state_conf,
           state_gdn, state_gdn_conv, state_ffn_conv, page_table, mem_prompt, norm_mix, norm_mem, norm_ffn,
           norm_final, conf_w_pw1, conf_w_dw, conf_ln_g, conf_ln_b, conf_w_pw2, fox_w_in, fox_b_f, fox_w_o,
           gdn_w_in, gdn_w_conv, gdn_a_log, gdn_dt_bias, gdn_norm_g, gdn_w_o, x_w_q, x_w_kv, x_w_o, ffn_w_up,
           ffn_w_dw, ffn_w_down):
    W = dict(norm_mix=norm_mix, norm_mem=norm_mem, norm_ffn=norm_ffn, norm_final=norm_final,
             conf_w_pw1=conf_w_pw1, conf_w_dw=conf_w_dw, conf_ln_g=conf_ln_g, conf_ln_b=conf_ln_b,
             conf_w_pw2=conf_w_pw2, fox_w_in=fox_w_in, fox_b_f=fox_b_f, fox_w_o=fox_w_o,
             gdn_w_in=gdn_w_in, gdn_w_conv=gdn_w_conv, gdn_a_log=gdn_a_log, gdn_dt_bias=gdn_dt_bias,
             gdn_norm_g=gdn_norm_g, gdn_w_o=gdn_w_o, x_w_q=x_w_q, x_w_o=x_w_o,
             ffn_w_up=ffn_w_up, ffn_w_dw=ffn_w_dw, ffn_w_down=ffn_w_down)
    b, t, d = x_prompt.shape
    bs, ts, _ = x_sample.shape
    depth = x_w_kv.shape[0]
    mem = mem_prompt.shape[1]
    xw = x_w_kv.shape[2] // 2
    xh = xw // HEAD

    mem_rows = mem_prompt.reshape(b * mem, d)
    kv = jnp.stack([matmul(mem_rows, x_w_kv, i) for i in range(depth)])
    mem_k_p = kv[..., :xw].reshape(depth, b, mem, xw)
    mem_v_p = kv[..., xw:].reshape(depth, b, mem, xw)

    states = dict(conf=state_conf, gdn=state_gdn, gdn_conv=state_gdn_conv, ffn_conv=state_ffn_conv,
                  fox=(cache_fox_k, cache_fox_v, cache_fox_logf, page_table))
    mem_s = (cache_mem_k.reshape(depth, bs, mem, xw), cache_mem_v.reshape(depth, bs, mem, xw))
    ((y_p, conf_p, fox_k_p, fox_v_p, fox_lf_p, gdn_p, gdn_conv_p, ffn_conv_p),
     (y_s, conf_s, fox_k_s, fox_v_s, fox_lf_s, gdn_s, gdn_conv_s, ffn_conv_s)) = _trunk(
        x_prompt.reshape(b * t, d), b, x_sample.reshape(bs * ts, d), bs, (mem_k_p, mem_v_p), mem_s, states, W)

    return (y_p, y_s,
            conf_p, fox_k_p, fox_v_p, fox_lf_p, gdn_p, gdn_conv_p,
            mem_k_p.reshape(depth, b, mem, xh, HEAD), mem_v_p.reshape(depth, b, mem, xh, HEAD), ffn_conv_p,
            conf_s, fox_k_s, fox_v_s, fox_lf_s, gdn_s, gdn_conv_s, ffn_conv_s)
```
